```python
import jax, jax.numpy as jnp
from jax import lax
import numpy as np

D_MODEL = 1024
BATCH = 16
SEQ = 256
DEPTH = 4
DEC_BATCH = 8
DEC_SEQ = 2048
PAST_LEN = 512

GRID_W = 64
N_MIXERS = 3
N_A = (DEPTH + 2) // 3
N_B = (DEPTH + 1) // 3
N_C = DEPTH // 3
CHUNK = 64
EPS = 1e-6

A_INNER = 2 * D_MODEL
A_HEADS = 4
A_DH = A_INNER // A_HEADS
A_CONV = 3
A_BLK = 4
A_NBLK = A_INNER // A_BLK

B_HEADS = 4
B_DK = D_MODEL // 2
B_DV = D_MODEL
B_DKH = B_DK // B_HEADS
B_DVH = B_DV // B_HEADS
B_RANK = 16
B_TAU = 16.0

C_HEADS = 16
C_DH = D_MODEL // C_HEADS
C_WIN_H = 8
C_WIN_W = 16

kernel_name = 'bidir_mlstm_gla_natten_prefix_step'

F32 = jnp.float32


def _rmsnorm(x, g):
    xf = x.astype(F32)
    y = xf * lax.rsqrt(jnp.mean(xf * xf, axis=-1, keepdims=True) + EPS)
    return (y * g.astype(F32)).astype(x.dtype)


def _layernorm(x, g):
    xf = x.astype(F32)
    mu = jnp.mean(xf, axis=-1, keepdims=True)
    xc = xf - mu
    y = xc * lax.rsqrt(jnp.mean(xc * xc, axis=-1, keepdims=True) + EPS)
    return (y * g.astype(F32)).astype(x.dtype)


def _modulate(x, g, mod):
    shift, scale, gate = jnp.split(mod, 3, axis=-1)
    return _rmsnorm(x, g) * (1 + scale) + shift, gate


def _heads(x, n, dh):
    b, t, _ = x.shape
    return x.reshape(b, t, n, dh).transpose(0, 2, 1, 3)


def _flip(a, d):
    return jnp.flip(a, axis=2) if d == 1 else a


def _to_chunks(a):
    b, h, t = a.shape[:3]
    a = a.reshape((b, h, t // CHUNK, CHUNK) + a.shape[3:])
    return jnp.moveaxis(a, 2, 0)


def _from_chunks(a):
    a = jnp.moveaxis(a, 0, 2)
    return a.reshape(a.shape[:2] + (a.shape[2] * a.shape[3],) + a.shape[4:])


def _conv_centred(x, w, b):
    y = lax.conv_general_dilated(x, w[:, None, :].astype(x.dtype), window_strides=(1,), padding='SAME',
                                 dimension_numbers=('NWC', 'WIO', 'NWC'), feature_group_count=x.shape[-1])
    return y + b


def _headwise(x, w):
    b, t, _ = x.shape
    return jnp.einsum('btni,nio->btno', x.reshape(b, t, A_NBLK, A_BLK), w).reshape(b, t, A_INNER)


def _mlstm_scan(q, k, v, ig, lf, C0, n0, m0):
    dt = q.dtype
    mask = jnp.tril(jnp.ones((CHUNK, CHUNK), bool))

    def step(carry, inp):
        C, n, m = carry
        qc, kc, vc, igc, lfc = inp
        b = jnp.cumsum(lfc, axis=-1)
        dmat = jnp.where(mask, b[..., :, None] - b[..., None, :] + igc[..., None, :], -jnp.inf)
        inter = b + m[..., None]
        m_t = jnp.maximum(inter, jnp.max(dmat, axis=-1))
        s = jnp.einsum('bhtd,bhsd->bhts', qc, kc) * jnp.exp(dmat - m_t[..., None])
        w_inter = jnp.exp(inter - m_t)
        num = jnp.einsum('bhts,bhsv->bhtv', s, vc) + w_inter[..., None] * jnp.einsum('bhtd,bhdv->bhtv', qc, C)
        den = jnp.sum(s, axis=-1) + w_inter * jnp.einsum('bhtd,bhd->bht', qc, n)
        h = num / jnp.maximum(jnp.abs(den), jnp.exp(-m_t))[..., None]
        bL = b[..., -1]
        w_state = bL[..., None] - b + igc
        m_new = jnp.maximum(bL + m, jnp.max(w_state, axis=-1))
        decay = jnp.exp(bL + m - m_new)
        ws = jnp.exp(w_state - m_new[..., None])
        C_new = decay[..., None, None] * C + jnp.einsum('bhsd,bhsv->bhdv', kc * ws[..., None], vc)
        n_new = decay[..., None] * n + jnp.einsum('bhs,bhsd->bhd', ws, kc)
        return (C_new, n_new, m_new), h

    xs = tuple(_to_chunks(a.astype(F32)) for a in (q, k, v, ig, lf))
    (C, n, m), h = lax.scan(step, (C0.astype(F32), n0.astype(F32), m0.astype(F32)), xs)
    return (C.astype(dt), n.astype(dt), m.astype(dt)), _from_chunks(h).astype(dt)


def _mlstm_mixer(h, C0, n0, m0, w_in, conv_w, conv_b, wq, wk, wv, w_ig, b_ig, w_fg, b_fg, out_g, skip, w_out):
    bsz, t, _ = h.shape
    xm, z = jnp.split(h @ w_in, 2, axis=-1)
    xc = jax.nn.silu(_conv_centred(xm, conv_w, conv_b))
    q = _headwise(xc, wq)
    k = _headwise(xc, wk)
    v = _headwise(xm, wv)
    gin = jnp.concatenate([q, k, v], axis=-1)
    ipre = (jnp.einsum('btc,dch->dbht', gin, w_ig) + b_ig[:, None, :, None]).astype(F32)
    lf = jax.nn.log_sigmoid((jnp.einsum('btc,dch->dbht', gin, w_fg) + b_fg[:, None, :, None]).astype(F32))
    qh = _heads(q, A_HEADS, A_DH) * (A_DH ** -0.5)
    kh = _heads(k, A_HEADS, A_DH)
    vh = _heads(v, A_HEADS, A_DH)
    outs, Cs, ns, ms = [], [], [], []
    for d in range(2):
        (Cd, nd, md), hd = _mlstm_scan(_flip(qh, d), _flip(kh, d), _flip(vh, d), _flip(ipre[d], d),
                                       _flip(lf[d], d), C0[:, d], n0[:, d], m0[:, d])
        outs.append(_flip(hd, d))
        Cs.append(Cd)
        ns.append(nd)
        ms.append(md)
    hsum = (outs[0] + outs[1]).transpose(0, 2, 1, 3)
    hn = _layernorm(hsum, out_g.reshape(A_HEADS, A_DH)).reshape(bsz, t, A_INNER)
    y = ((hn + skip * xc) * jax.nn.silu(z)) @ w_out
    return y, jnp.stack(Cs, axis=1), jnp.stack(ns, axis=1), jnp.stack(ms, axis=1)


def _gla_scan(q, k, v, la, S0):
    dt = q.dtype
    mask = jnp.tril(jnp.ones((CHUNK, CHUNK), bool))

    def step(S, inp):
        qc, kc, vc, lac = inp
        b = jnp.cumsum(lac, axis=2)
        diff = jnp.where(mask[:, :, None], b[:, :, :, None, :] - b[:, :, None, :, :], -jnp.inf)
        att = jnp.einsum('bhtd,bhsd,bhtsd->bhts', qc, kc, jnp.exp(diff))
        o = jnp.einsum('bhts,bhsv->bhtv', att, vc) + jnp.einsum('bhtd,bhdv->bhtv', qc * jnp.exp(b), S)
        bL = b[:, :, -1:, :]
        S_new = jnp.exp(bL[:, :, 0, :])[..., None] * S + jnp.einsum('bhsd,bhsv->bhdv', kc * jnp.exp(bL - b), vc)
        return S_new, o

    xs = tuple(_to_chunks(a.astype(F32)) for a in (q, k, v, la))
    S, o = lax.scan(step, S0.astype(F32), xs)
    return S.astype(dt), _from_chunks(o).astype(dt)


def _gla_mixer(h, S0, w_in, w_a1, w_a2, b_a, out_g, w_out):
    bsz, t, _ = h.shape
    q, k, v, r = jnp.split(h @ w_in, [B_DK, 2 * B_DK, 2 * B_DK + B_DV], axis=-1)
    qh = _heads(q, B_HEADS, B_DKH) * (B_DKH ** -0.5)
    kh = _heads(k, B_HEADS, B_DKH)
    vh = _heads(v, B_HEADS, B_DVH)
    outs, Ss = [], []
    for d in range(2):
        la = jax.nn.log_sigmoid(((h @ w_a1[d]) @ w_a2[d] + b_a[d]).astype(F32)) / B_TAU
        la = _heads(la, B_HEADS, B_DKH)
        Sd, od = _gla_scan(_flip(qh, d), _flip(kh, d), _flip(vh, d), _flip(la, d), S0[:, d])
        outs.append(_flip(od, d))
        Ss.append(Sd)
    o = (outs[0] + outs[1]).transpose(0, 2, 1, 3)
    o = _rmsnorm(o, out_g.reshape(B_HEADS, B_DVH)).reshape(bsz, t, B_DV)
    return (o * jax.nn.silu(r)) @ w_out, jnp.stack(Ss, axis=1)


def _na_project(h, w_in, q_g, k_g):
    q, k, v, g = jnp.split(h @ w_in, 4, axis=-1)
    q = _rmsnorm(_heads(q, C_HEADS, C_DH), q_g)
    k = _rmsnorm(_heads(k, C_HEADS, C_DH), k_g)
    return q, k, _heads(v, C_HEADS, C_DH), g


def _na_context(h, w_in, q_g, k_g, w_out):
    bsz, t, _ = h.shape
    q, k, v, g = _na_project(h, w_in, q_g, k_g)
    s = jnp.einsum('bhqd,bhkd->bhqk', q, k).astype(F32) * (C_DH ** -0.5)
    p = jax.nn.softmax(s, axis=-1).astype(v.dtype)
    o = jnp.einsum('bhqk,bhkd->bhqd', p, v).transpose(0, 2, 1, 3).reshape(bsz, t, D_MODEL)
    return (o * jax.nn.silu(g)) @ w_out, k, v


def _na_latent(h, k_ctx, v_ctx, w_in, q_g, k_g, rpb, w_out):
    bsz, t, _ = h.shape
    rows = t // GRID_W
    win_h = min(C_WIN_H, rows)
    q, k, v, g = _na_project(h, w_in, q_g, k_g)
    qg = q.reshape(bsz, C_HEADS, rows, GRID_W, C_DH)
    kg = k.reshape(bsz, C_HEADS, rows, GRID_W, C_DH)
    vg = v.reshape(bsz, C_HEADS, rows, GRID_W, C_DH)
    cols = jnp.arange(GRID_W)
    col_start = jnp.clip(cols - C_WIN_W // 2, 0, GRID_W - C_WIN_W)
    col_idx = col_start[:, None] + jnp.arange(C_WIN_W)[None, :]
    col_rel = col_idx - cols[:, None] + (C_WIN_W - 1)
    rpb_c = rpb[:, :, col_rel]
    scale = C_DH ** -0.5
    n_loc = win_h * C_WIN_W

    def row_block(r):
        r0 = jnp.clip(r - win_h // 2, 0, rows - win_h)
        q_r = lax.dynamic_index_in_dim(qg, r, axis=2, keepdims=False)
        k_r = lax.dynamic_slice_in_dim(kg, r0, win_h, axis=2)
        v_r = lax.dynamic_slice_in_dim(vg, r0, win_h, axis=2)
        k_w = k_r[:, :, :, col_idx]
        v_w = v_r[:, :, :, col_idx]
        row_rel = r0 + jnp.arange(win_h) - r + (C_WIN_H - 1)
        bias = rpb_c[:, row_rel].transpose(0, 2, 1, 3).astype(F32)
        s_loc = jnp.einsum('bhqd,bhrqwd->bhqrw', q_r, k_w).astype(F32) * scale + bias[None]
        s_ctx = jnp.einsum('bhqd,bhcd->bhqc', q_r, k_ctx).astype(F32) * scale
        s = jnp.concatenate([s_loc.reshape(bsz, C_HEADS, GRID_W, n_loc), s_ctx], axis=-1)
        p = jax.nn.softmax(s, axis=-1).astype(v.dtype)
        p_loc = p[..., :n_loc].reshape(bsz, C_HEADS, GRID_W, win_h, C_WIN_W)
        return (jnp.einsum('bhqrw,bhrqwd->bhqd', p_loc, v_w)
                + jnp.einsum('bhqc,bhcd->bhqd', p[..., n_loc:], v_ctx))

    o = lax.map(row_block, jnp.arange(rows))
    o = o.transpose(1, 0, 3, 2, 4).reshape(bsz, t, D_MODEL)
    return (o * jax.nn.silu(g)) @ w_out


def setup_inputs(seed: int = 0) -> dict:
    key = jax.random.key(seed)
    ks = iter(jax.random.split(key, 48))

    def nrm(shape, scale):
        return jax.random.normal(next(ks), shape, F32) * scale

    def gain(shape):
        return 1.0 + nrm(shape, 0.01)

    D = D_MODEL
    return {
        'x_prompt': nrm((BATCH, SEQ, D), 1.0),
        'x_sample': nrm((DEC_BATCH, DEC_SEQ, D), 1.0),
        'state_mlstm_C': nrm((DEC_BATCH, N_A, 2, A_HEADS, A_DH, A_DH), 0.1),
        'state_mlstm_n': nrm((DEC_BATCH, N_A, 2, A_HEADS, A_DH), 0.1),
        'state_mlstm_m': jax.random.uniform(next(ks), (DEC_BATCH, N_A, 2, A_HEADS), F32, -1.0, 1.0),
        'state_gla_S': nrm((DEC_BATCH, N_B, 2, B_HEADS, B_DKH, B_DVH), 1.0),
        'cache_na_k': nrm((DEC_BATCH, N_C, C_HEADS, PAST_LEN, C_DH), 1.0),
        'cache_na_v': nrm((DEC_BATCH, N_C, C_HEADS, PAST_LEN, C_DH), 1.0),
        'c': nrm((DEC_BATCH, D), 1.0),
        'c_ctx': nrm((D,), 1.0),
        'norm_g': gain((DEPTH, D)),
        'w_mod': nrm((DEPTH, D, 3 * D), 0.5 * D ** -0.5),
        'b_mod': nrm((DEPTH, 3 * D), 0.02),
        'a_w_in': nrm((N_A, D, 2 * A_INNER), D ** -0.5),
        'a_conv_w': nrm((N_A, A_CONV, A_INNER), A_CONV ** -0.5),
        'a_conv_b': nrm((N_A, A_INNER), 0.02),
        'a_wq': nrm((N_A, A_NBLK, A_BLK, A_BLK), A_BLK ** -0.5),
        'a_wk': nrm((N_A, A_NBLK, A_BLK, A_BLK), A_BLK ** -0.5),
        'a_wv': nrm((N_A, A_NBLK, A_BLK, A_BLK), A_BLK ** -0.5),
        'a_w_ig': nrm((N_A, 2, 3 * A_INNER, A_HEADS), (3 * A_INNER) ** -0.5),
        'a_b_ig': nrm((N_A, 2, A_HEADS), 0.1),
        'a_w_fg': nrm((N_A, 2, 3 * A_INNER, A_HEADS), (3 * A_INNER) ** -0.5),
        'a_b_fg': jnp.linspace(3.0, 6.0, A_HEADS, dtype=F32) + nrm((N_A, 2, A_HEADS), 0.1),
        'a_out_g': gain((N_A, A_INNER)),
        'a_skip': gain((N_A, A_INNER)),
        'a_w_out': nrm((N_A, A_INNER, D), A_INNER ** -0.5),
        'b_w_in': nrm((N_B, D, 2 * B_DK + 2 * B_DV), D ** -0.5),
        'b_w_a1': nrm((N_B, 2, D, B_RANK), D ** -0.5),
        'b_w_a2': nrm((N_B, 2, B_RANK, B_DK), B_RANK ** -0.5),
        'b_b_a': nrm((N_B, 2, B_DK), 0.1),
        'b_out_g': gain((N_B, B_DV)),
        'b_w_out': nrm((N_B, B_DV, D), B_DV ** -0.5),
        'c_w_in': nrm((N_C, D, 4 * D), D ** -0.5),
        'c_q_g': gain((N_C, C_DH)),
        'c_k_g': gain((N_C, C_DH)),
        'c_rpb': nrm((N_C, C_HEADS, 2 * C_WIN_H - 1, 2 * C_WIN_W - 1), 0.1),
        'c_w_out': nrm((N_C, D, D), D ** -0.5),
    }


def reference(x_prompt, x_sample, state_mlstm_C, state_mlstm_n, state_mlstm_m, state_gla_S, cache_na_k,
              cache_na_v, c, c_ctx, norm_g, w_mod, b_mod, a_w_in, a_conv_w, a_conv_b, a_wq, a_wk, a_wv,
              a_w_ig, a_b_ig, a_w_fg, a_b_fg, a_out_g, a_skip, a_w_out, b_w_in, b_w_a1, b_w_a2, b_b_a,
              b_out_g, b_w_out, c_w_in, c_q_g, c_k_g, c_rpb, c_w_out):
    sc = jax.nn.silu(c)
    scc = jax.nn.silu(c_ctx)
    yp, ys = x_prompt, x_sample
    bp = x_prompt.shape[0]
    new_C, new_n, new_m, new_S, new_k, new_v = [], [], [], [], [], []
    for l in range(DEPTH):
        kind, j = l % N_MIXERS, l // N_MIXERS
        mod_p = (scc @ w_mod[l] + b_mod[l])[None, None, :]
        mod_s = (sc @ w_mod[l] + b_mod[l])[:, None, :]
        hp, gp = _modulate(yp, norm_g[l], mod_p)
        hs, gs = _modulate(ys, norm_g[l], mod_s)
        if kind == 0:
            pa = (a_w_in[j], a_conv_w[j], a_conv_b[j], a_wq[j], a_wk[j], a_wv[j], a_w_ig[j], a_b_ig[j],
                  a_w_fg[j], a_b_fg[j], a_out_g[j], a_skip[j], a_w_out[j])
            zC = jnp.zeros((bp, 2, A_HEADS, A_DH, A_DH), yp.dtype)
            zn = jnp.zeros((bp, 2, A_HEADS, A_DH), yp.dtype)
            zm = jnp.zeros((bp, 2, A_HEADS), yp.dtype)
            op, Cf, nf, mf = _mlstm_mixer(hp, zC, zn, zm, *pa)
            os_, _, _, _ = _mlstm_mixer(hs, state_mlstm_C[:, j], state_mlstm_n[:, j], state_mlstm_m[:, j], *pa)
            new_C.append(Cf)
            new_n.append(nf)
            new_m.append(mf)
        elif kind == 1:
            pb = (b_w_in[j], b_w_a1[j], b_w_a2[j], b_b_a[j], b_out_g[j], b_w_out[j])
            zS = jnp.zeros((bp, 2, B_HEADS, B_DKH, B_DVH), yp.dtype)
            op, Sf = _gla_mixer(hp, zS, *pb)
            os_, _ = _gla_mixer(hs, state_gla_S[:, j], *pb)
            new_S.append(Sf)
        else:
            op, kc, vc = _na_context(hp, c_w_in[j], c_q_g[j], c_k_g[j], c_w_out[j])
            os_ = _na_latent(hs, cache_na_k[:, j], cache_na_v[:, j], c_w_in[j], c_q_g[j], c_k_g[j],
                             c_rpb[j], c_w_out[j])
            new_k.append(kc)
            new_v.append(vc)
        yp = yp + gp * op
        ys = ys + gs * os_
    return (yp, ys, jnp.stack(new_C, axis=1), jnp.stack(new_n, axis=1), jnp.stack(new_m, axis=1),
            jnp.stack(new_S, axis=1), jnp.stack(new_k, axis=1), jnp.stack(new_v, axis=1))
```

```python
import functools

import jax
import jax.numpy as jnp
from jax import lax
from jax.experimental import pallas as pl
from jax.experimental.pallas import tpu as pltpu

F32 = jnp.float32
BF16 = jnp.bfloat16
EPS = 1e-6

GRID_W = 64
NA_WIN_H = 8
NA_WIN_W = 16
GLA_TAU = 16.0
GLA_RANK = 16
MLSTM_CHUNK = 256
GLA_CHUNK = 64
GLA_DIAG = 16
ROW_TILE = 512
V7X_VMEM_LIMIT = 56 * 1024 * 1024


def _params(sem):
    return pltpu.CompilerParams(dimension_semantics=sem, vmem_limit_bytes=V7X_VMEM_LIMIT)


def _dot(a, b):
    return jnp.dot(a, b, preferred_element_type=F32)


def _dot_nt(a, b):
    return lax.dot_general(a, b, (((1,), (1,)), ((), ())), preferred_element_type=F32)


def _dot_tn(a, b):
    return lax.dot_general(a, b, (((0,), (0,)), ((), ())), preferred_element_type=F32)


def _split3(x):
    hi = x.astype(BF16)
    r1 = x - hi.astype(F32)
    mid = r1.astype(BF16)
    lo = (r1 - mid.astype(F32)).astype(BF16)
    return hi, mid, lo


def _dot_exact_lhs(sel, x):
    hi, mid, lo = _split3(x)
    return _dot(sel, hi) + (_dot(sel, mid) + _dot(sel, lo))


def _silu(x):
    return x * jax.nn.sigmoid(x)


def _log_sigmoid(x):
    return jnp.minimum(x, 0.0) - jnp.log(1.0 + jnp.exp(-jnp.abs(x)))


def _mod_kernel(c_ref, w_ref, b_ref, o_ref):
    a = _silu(c_ref[...])
    w = w_ref[0]
    ah = a.astype(BF16)
    al = (a - ah.astype(F32)).astype(BF16)
    wh = w.astype(BF16)
    wl = (w - wh.astype(F32)).astype(BF16)
    o_ref[0] = _dot(ah, wh) + (_dot(al, wh) + _dot(ah, wl)) + b_ref[0]


def _modulation(cs, w_mod, b_mod):
    depth, d, d3 = w_mod.shape
    rows = cs.shape[0]
    return pl.pallas_call(
        _mod_kernel,
        grid=(depth, d3 // d),
        in_specs=[
            pl.BlockSpec((rows, d), lambda l, j: (0, 0)),
            pl.BlockSpec((1, d, d), lambda l, j: (l, 0, j)),
            pl.BlockSpec((1, 1, d), lambda l, j: (l, 0, j)),
        ],
        out_specs=pl.BlockSpec((1, rows, d), lambda l, j: (l, 0, j)),
        out_shape=jax.ShapeDtypeStruct((depth, rows, d3), F32),
        compiler_params=_params(("parallel", "parallel")),
        name="modulation",
    )(cs, w_mod, b_mod.reshape(depth, 1, d3))


def _inproj_kernel(x_ref, mod_ref, g_ref, w_ref, o_ref, *, d):
    x = x_ref[...]
    y = x * lax.rsqrt(jnp.mean(x * x, axis=-1, keepdims=True) + EPS) * g_ref[...]
    shift = mod_ref[0, :, 0:d]
    scale = mod_ref[0, :, d:2 * d]
    h = y * (1.0 + scale) + shift
    o_ref[...] = _dot(h.astype(BF16), w_ref[...]).astype(o_ref.dtype)


def _inproj(x, mod, g, w, seq_len):
    m, d = x.shape
    n = w.shape[1]
    tm = min(ROW_TILE, seq_len)
    tiles_per_seq = seq_len // tm
    if mod.shape[0] == 1:
        mod_map = lambda i: (0, 0, 0)
    else:
        mod_map = lambda i: (i // tiles_per_seq, 0, 0)
    return pl.pallas_call(
        functools.partial(_inproj_kernel, d=d),
        grid=(m // tm,),
        in_specs=[
            pl.BlockSpec((tm, d), lambda i: (i, 0)),
            pl.BlockSpec((1, 1, 3 * d), mod_map),
            pl.BlockSpec((1, d), lambda i: (0, 0)),
            pl.BlockSpec((d, n), lambda i: (0, 0)),
        ],
        out_specs=pl.BlockSpec((tm, n), lambda i: (i, 0)),
        out_shape=jax.ShapeDtypeStruct((m, n), F32),
        compiler_params=_params(("parallel",)),
        name="inproj",
    )(x, mod, g, w)


def _outproj_kernel(u_ref, r_ref, mod_ref, w_ref, y_ref, o_ref, *, d):
    a = u_ref[...] * _silu(r_ref[...])
    out = _dot(a.astype(BF16), w_ref[...])
    gate = mod_ref[0, :, 2 * d:3 * d]
    o_ref[...] = y_ref[...] + gate * out


def _outproj(u, r_arr, r_col_block, mod, w, y, seq_len):
    m, kdim = u.shape
    d = y.shape[1]
    tm = min(ROW_TILE, seq_len)
    tiles_per_seq = seq_len // tm
    if mod.shape[0] == 1:
        mod_map = lambda i: (0, 0, 0)
    else:
        mod_map = lambda i: (i // tiles_per_seq, 0, 0)
    return pl.pallas_call(
        functools.partial(_outproj_kernel, d=d),
        grid=(m // tm,),
        in_specs=[
            pl.BlockSpec((tm, kdim), lambda i: (i, 0)),
            pl.BlockSpec((tm, kdim), lambda i: (i, r_col_block)),
            pl.BlockSpec((1, 1, 3 * d), mod_map),
            pl.BlockSpec((kdim, d), lambda i: (0, 0)),
            pl.BlockSpec((tm, d), lambda i: (i, 0)),
        ],
        out_specs=pl.BlockSpec((tm, d), lambda i: (i, 0)),
        out_shape=jax.ShapeDtypeStruct((m, d), F32),
        compiler_params=_params(("parallel",)),
        name="outproj",
    )(u, r_arr, mod, w, y)


def _mlstm_pre_kernel(xm_ref, prev_ref, next_ref, cw_ref, cb_ref, bdq_ref, bdk_ref, bdv_ref, wg_ref, bg_ref,
                      q_ref, k_ref, v_ref, xc_ref, gc_ref, gr_ref, *, tiles_per_seq, n_ig):
    i = pl.program_id(0)
    tm, inner = xm_ref.shape
    xm = xm_ref[...]
    row = lax.broadcasted_iota(jnp.int32, (tm, 1), 0)
    first = (i % tiles_per_seq) == 0
    last = (i % tiles_per_seq) == tiles_per_seq - 1
    prev_row = jnp.where(first, 0.0, prev_ref[7:8, :])
    next_row = jnp.where(last, 0.0, next_ref[0:1, :])
    x_m1 = jnp.where(row == 0, prev_row, pltpu.roll(xm, 1, 0))
    x_p1 = jnp.where(row == tm - 1, next_row, pltpu.roll(xm, tm - 1, 0))
    conv = cw_ref[0:1, :] * x_m1 + cw_ref[1:2, :] * xm + cw_ref[2:3, :] * x_p1 + cb_ref[...]
    xc = _silu(conv)
    xc_ref[...] = xc

    xc_b = xc.astype(BF16)
    xm_b = xm.astype(BF16)
    blk = bdq_ref.shape[1]
    acc = jnp.zeros((tm, wg_ref.shape[1]), F32)
    for j in range(inner // blk):
        cols = slice(j * blk, (j + 1) * blk)
        qj = _dot(xc_b[:, cols], bdq_ref[j])
        kj = _dot(xc_b[:, cols], bdk_ref[j])
        vj = _dot(xm_b[:, cols], bdv_ref[j])
        qb, kb, vb = qj.astype(BF16), kj.astype(BF16), vj.astype(BF16)
        q_ref[:, cols] = qb
        k_ref[:, cols] = kb
        v_ref[:, cols] = vb
        acc = acc + _dot(qb, wg_ref[cols, :])
        acc = acc + _dot(kb, wg_ref[inner + j * blk:inner + (j + 1) * blk, :])
        acc = acc + _dot(vb, wg_ref[2 * inner + j * blk:2 * inner + (j + 1) * blk, :])
    g = acc + bg_ref[...]
    lane = lax.broadcasted_iota(jnp.int32, g.shape, 1)
    lf = jnp.where((lane >= n_ig) & (lane < 2 * n_ig), _log_sigmoid(g), 0.0)
    r = lax.broadcasted_iota(jnp.int32, (tm, tm), 0)
    c = lax.broadcasted_iota(jnp.int32, (tm, tm), 1)
    prefix = _dot_exact_lhs((c <= r).astype(BF16), lf)
    suffix = _dot_exact_lhs((c >= r).astype(BF16), lf)
    half = n_ig + n_ig // 2
    gcol = jnp.where(lane < n_ig, g, jnp.where(lane < half, prefix, suffix))
    gc_ref[...] = gcol[:, 0:2 * n_ig]
    gr_ref[...] = jnp.transpose(gcol)[0:2 * n_ig, :]


def _mlstm_pre(xz, conv_w, conv_b, bdq, bdk, bdv, wg, bg, seq_len, n_heads):
    m = xz.shape[0]
    inner = conv_w.shape[1]
    tm = MLSTM_CHUNK
    tiles_per_seq = seq_len // tm
    n_ig = 2 * n_heads
    last8 = m // 8 - 1
    return pl.pallas_call(
        functools.partial(_mlstm_pre_kernel, tiles_per_seq=tiles_per_seq, n_ig=n_ig),
        grid=(m // tm,),
        in_specs=[
            pl.BlockSpec((tm, inner), lambda i: (i, 0)),
            pl.BlockSpec((8, inner), lambda i: (jnp.maximum(i * (tm // 8) - 1, 0), 0)),
            pl.BlockSpec((8, inner), lambda i: (jnp.minimum((i + 1) * (tm // 8), last8), 0)),
            pl.BlockSpec(conv_w.shape, lambda i: (0, 0)),
            pl.BlockSpec((1, inner), lambda i: (0, 0)),
            pl.BlockSpec(bdq.shape, lambda i: (0, 0, 0)),
            pl.BlockSpec(bdk.shape, lambda i: (0, 0, 0)),
            pl.BlockSpec(bdv.shape, lambda i: (0, 0, 0)),
            pl.BlockSpec(wg.shape, lambda i: (0, 0)),
            pl.BlockSpec(bg.shape, lambda i: (0, 0)),
        ],
        out_specs=[
            pl.BlockSpec((tm, inner), lambda i: (i, 0)),
            pl.BlockSpec((tm, inner), lambda i: (i, 0)),
            pl.BlockSpec((tm, inner), lambda i: (i, 0)),
            pl.BlockSpec((tm, inner), lambda i: (i, 0)),
            pl.BlockSpec((tm, 2 * n_ig), lambda i: (i, 0)),
            pl.BlockSpec((2 * n_ig, tm), lambda i: (0, i)),
        ],
        out_shape=[
            jax.ShapeDtypeStruct((m, inner), BF16),
            jax.ShapeDtypeStruct((m, inner), BF16),
            jax.ShapeDtypeStruct((m, inner), BF16),
            jax.ShapeDtypeStruct((m, inner), F32),
            jax.ShapeDtypeStruct((m, 2 * n_ig), F32),
            jax.ShapeDtypeStruct((2 * n_ig, m), F32),
        ],
        compiler_params=_params(("parallel",)),
        name="mlstm_pre",
    )(xz, xz, xz, conv_w, conv_b, bdq, bdk, bdv, wg, bg)


def _mlstm_scan_kernel(*refs, n_chunks, has_state, emit_state):
    q_ref, k_ref, v_ref, xc_ref, gc_ref, gr_ref, og_ref, skip_ref = refs[:8]
    pos = 8
    if has_state:
        c0_ref, n0_ref, m0_ref = refs[pos:pos + 3]
        pos += 3
    u_ref = refs[pos]
    pos += 1
    if emit_state:
        cn_ref, nn_ref, mn_ref = refs[pos:pos + 3]
        pos += 3
    c_scr, n_scr, m_scr = refs[pos:pos + 3]

    L = MLSTM_CHUNK
    dh = q_ref.shape[1]
    scale = dh ** -0.5
    use_inter = has_state or n_chunks > 1
    row = lax.broadcasted_iota(jnp.int32, (L, L), 0)
    col = lax.broadcasted_iota(jnp.int32, (L, L), 1)

    def chunk(c, d, finish):
        rows = pl.ds(pl.multiple_of(c * L, L), L)
        q = q_ref[rows, :]
        k = k_ref[rows, :]
        v = v_ref[rows, :]
        gc = gc_ref[0, rows, :]
        gr = gr_ref[0, :, rows]
        ig_col, b_col = gc[:, d:d + 1], gc[:, 2 + d:3 + d]
        ig_row, b_row = gr[d:d + 1, :], gr[2 + d:3 + d, :]
        mask = (col <= row) if d == 0 else (col >= row)
        dmat = jnp.where(mask, b_col - b_row + ig_row, -jnp.inf)
        m_intra = jnp.max(dmat, axis=1, keepdims=True)
        if use_inter:
            m_prev = m_scr[...]
            inter = b_col + m_prev
            m_t = jnp.maximum(inter, m_intra)
        else:
            m_t = m_intra
        s = _dot_nt(q, k) * (scale * jnp.exp(dmat - m_t))
        num = _dot(s.astype(BF16), v)
        den = jnp.sum(s, axis=1, keepdims=True)
        if use_inter:
            w_inter = scale * jnp.exp(inter - m_t)
            num = num + w_inter * _dot(q, c_scr[...].astype(BF16))
            den = den + w_inter * jnp.sum(q.astype(F32) * n_scr[...], axis=1, keepdims=True)
        h = num / jnp.maximum(jnp.abs(den), jnp.exp(-m_t))
        if not finish:
            u_ref[rows, :] = h
        else:
            hs = u_ref[rows, :] + h
            mu = jnp.mean(hs, axis=-1, keepdims=True)
            hc = hs - mu
            hn = hc * lax.rsqrt(jnp.mean(hc * hc, axis=-1, keepdims=True) + EPS) * og_ref[...]
            u_ref[rows, :] = hn + skip_ref[...] * xc_ref[rows, :]

        def update():
            b_last = b_col[L - 1:L, :] if d == 0 else b_col[0:1, :]
            w_state = b_last - b_col + ig_col
            m_intra_s = jnp.max(w_state, axis=0, keepdims=True)
            if use_inter:
                m_new = jnp.maximum(b_last + m_prev, m_intra_s)
                decay = jnp.exp(b_last + m_prev - m_new)
            else:
                m_new = jnp.maximum(b_last, m_intra_s)
            kw =k.astype(F32) * jnp.exp(w_state - m_new)
            c_add = _dot_tn(kw.astype(BF16), v)
            n_add = jnp.sum(kw, axis=0, keepdims=True)
            if use_inter:
                c_scr[...] = decay * c_scr[...] + c_add
                n_scr[...] = decay * n_scr[...] + n_add
            else:
                c_scr[...] = c_add
                n_scr[...] = n_add
            m_scr[...] = m_new

        return update

    for d in range(2):
        if has_state:
            c_scr[...] = c0_ref[0, 0, d, 0]
            n_scr[...] = n0_ref[0, 0, d, 0]
            m_scr[...] = m0_ref[0, 0, d, 0][:, 0:1]
        elif use_inter:
            c_scr[...] = jnp.zeros_like(c_scr)
            n_scr[...] = jnp.zeros_like(n_scr)
            m_scr[...] = jnp.zeros_like(m_scr)

        def body(step, carry, d=d):
            c = step if d == 0 else n_chunks - 1 - step
            update = chunk(c, d, finish=(d == 1))
            if emit_state:
                update()
            else:
                pl.when(step < n_chunks - 1)(update)
            return carry

        if n_chunks == 1:
            body(0, 0)
        else:
            lax.fori_loop(0, n_chunks, body, 0)
        if emit_state:
            cn_ref[0, d, 0] = c_scr[...]
            nn_ref[0, d, 0] = n_scr[...]
            mn_ref[0, d, 0] = jnp.broadcast_to(m_scr[...], mn_ref.shape[3:])


def _mlstm_scan(q, k, v, xc, gch, grh, out_g, skip, state, layer_idx, n_seq, seq_len, n_heads, emit_state):
    m, inner = q.shape
    dh = inner // n_heads
    n_chunks = seq_len // MLSTM_CHUNK
    has_state = state is not None
    tok = lambda b, h: (b, h)
    in_specs = [
        pl.BlockSpec((seq_len, dh), tok),
        pl.BlockSpec((seq_len, dh), tok),
        pl.BlockSpec((seq_len, dh), tok),
        pl.BlockSpec((seq_len, dh), tok),
        pl.BlockSpec((1, seq_len, 4), lambda b, h: (h, b, 0)),
        pl.BlockSpec((1, 4, seq_len), lambda b, h: (h, 0, b)),
        pl.BlockSpec((1, dh), lambda b, h: (0, h)),
        pl.BlockSpec((1, dh), lambda b, h: (0, h)),
    ]
    args = [q, k, v, xc, gch, grh, out_g, skip]
    if has_state:
        c0, n0, m0 = state
        j = layer_idx
        in_specs += [
            pl.BlockSpec((1, 1, 2, 1, dh, dh), lambda b, h: (b, j, 0, h, 0, 0)),
            pl.BlockSpec((1, 1, 2, 1, 1, dh), lambda b, h: (b, j, 0, h, 0, 0)),
            pl.BlockSpec((1, 1, 2, 1, 1, 128), lambda b, h: (b, j, 0, h, 0, 0)),
        ]
        args += [c0, n0, m0]
    out_specs = [pl.BlockSpec((seq_len, dh), tok)]
    out_shape = [jax.ShapeDtypeStruct((m, inner), F32)]
    if emit_state:
        out_specs += [
            pl.BlockSpec((1, 2, 1, dh, dh), lambda b, h: (b, 0, h, 0, 0)),
            pl.BlockSpec((1, 2, 1, 1, dh), lambda b, h: (b, 0, h, 0, 0)),
            pl.BlockSpec((1, 2, 1, 1, 128), lambda b, h: (b, 0, h, 0, 0)),
        ]
        out_shape += [
            jax.ShapeDtypeStruct((n_seq, 2, n_heads, dh, dh), F32),
            jax.ShapeDtypeStruct((n_seq, 2, n_heads, 1, dh), F32),
            jax.ShapeDtypeStruct((n_seq, 2, n_heads, 1, 128), F32),
        ]
    return pl.pallas_call(
        functools.partial(_mlstm_scan_kernel, n_chunks=n_chunks, has_state=has_state, emit_state=emit_state),
        grid=(n_seq, n_heads),
        in_specs=in_specs,
        out_specs=out_specs,
        out_shape=out_shape,
        scratch_shapes=[pltpu.VMEM((dh, dh), F32), pltpu.VMEM((1, dh), F32), pltpu.VMEM((1, 1), F32)],
        compiler_params=_params(("parallel", "parallel")),
        name="mlstm_scan",
    )(*args)


def _gla_scan_kernel(*refs, n_chunks, has_state, emit_state):
    q_ref, k_ref, v_ref, a_ref, w2_ref, ba_ref, og_ref, dsel_ref = refs[:8]
    pos = 8
    if has_state:
        s0_ref = refs[pos]
        pos += 1
    o_ref = refs[pos]
    pos += 1
    if emit_state:
        sn_ref = refs[pos]
        pos += 1
    st_scr = refs[pos]

    L = GLA_CHUNK
    DB = GLA_DIAG
    dk = q_ref.shape[1]
    qscale = dk ** -0.5
    row = lax.broadcasted_iota(jnp.int32, (L, L), 0)
    col = lax.broadcasted_iota(jnp.int32, (L, L), 1)
    sub = lax.broadcasted_iota(jnp.int32, (L // DB, DB, 1), 1)

    def chunk(c, d, finish):
        rev = d == 1
        rows = pl.ds(pl.multiple_of(c * L, L), L)
        q = q_ref[rows, :] * qscale
        k = k_ref[rows, :]
        v = v_ref[rows, :].astype(BF16)
        la = _log_sigmoid(_dot(a_ref[rows, :].astype(BF16), w2_ref[d]) + ba_ref[d]) * (1.0 / GLA_TAU)
        tri = ((col >= row) if rev else (col <= row)).astype(BF16)
        b = _dot_exact_lhs(tri, la)

        o = _dot_nt((q * jnp.exp(b)).astype(BF16), st_scr[...].astype(BF16))

        att = jnp.zeros((L, L), F32)
        blk = L // 2
        while blk >= DB:
            pieces = []
            for g in range(L // (2 * blk)):
                idx = g * 2 * blk + (blk if rev else blk - 1)
                pieces.append(jnp.broadcast_to(b[idx:idx + 1, :], (2 * blk, dk)))
            ref_rows = pieces[0] if len(pieces) == 1 else jnp.concatenate(pieces, axis=0)
            e = jnp.exp(-jnp.abs(b - ref_rows))
            a_lvl = _dot_nt((q * e).astype(BF16), (k * e).astype(BF16))
            rb, cb = row // blk, col // blk
            if rev:
                pair = ((rb % 2) == 0) & (cb == rb + 1)
            else:
                pair = ((rb % 2) == 1) & (cb == rb - 1)
            att = jnp.where(pair, a_lvl, att)
            blk //= 2

        q3 = q.reshape(L // DB, DB, dk)
        k3 = k.reshape(L // DB, DB, dk)
        b3 = b.reshape(L // DB, DB, dk)
        parts = []
        for j in range(DB):
            valid = (sub <= j) if rev else (sub >= j)
            x = q3 * k3[:, j:j + 1, :] * jnp.exp(jnp.where(valid, b3 - b3[:, j:j + 1, :], -jnp.inf))
            parts.append(x.reshape(L, dk).astype(BF16))
        diag = _dot(jnp.concatenate(parts, axis=1), dsel_ref[...])
        att = jnp.where((row // DB) == (col // DB), diag, att)

        o = o + _dot(att.astype(BF16), v)
        if not finish:
            o_ref[rows, :] = o
        else:
            os_ = o_ref[rows, :] + o
            o_ref[rows, :] = os_ * lax.rsqrt(jnp.mean(os_ * os_, axis=-1, keepdims=True) + EPS) * og_ref[...]

        def update():
            b_last = b[0:1, :] if rev else b[L - 1:L, :]
            kt = (k * jnp.exp(b_last - b)).astype(BF16)
            st_scr[...] = st_scr[...] * jnp.exp(b_last) + _dot_tn(v, kt)

        return update

    for d in range(2):
        if has_state:
            st_scr[...] = jnp.transpose(s0_ref[0, 0, d, 0])
        else:
            st_scr[...] = jnp.zeros_like(st_scr)

        def body(step, carry, d=d):
            c = step if d == 0 else n_chunks - 1 - step
            update = chunk(c, d, finish=(d == 1))
            if emit_state:
                update()
            else:
                pl.when(step < n_chunks - 1)(update)
            return carry

        lax.fori_loop(0, n_chunks, body, 0)
        if emit_state:
            sn_ref[0, 0, d, 0] = jnp.transpose(st_scr[...])


def _gla_scan(proj, w2pad, b_a, out_g, dsel, state, n_seq, seq_len, n_heads, dk_total, dv_total, emit_state):
    m = proj.shape[0]
    dk = dk_total // n_heads
    dv = dv_total // n_heads
    n_chunks = seq_len // GLA_CHUNK
    has_state = state is not None
    a_block = (2 * dk_total + 2 * dv_total) // 128
    in_specs = [
        pl.BlockSpec((seq_len, dk), lambda b, h: (b, h)),
        pl.BlockSpec((seq_len, dk), lambda b, h: (b, n_heads + h)),
        pl.BlockSpec((seq_len, dv), lambda b, h: (b, 2 * dk_total // dv + h)),
        pl.BlockSpec((seq_len, 128), lambda b, h: (b, a_block)),
        pl.BlockSpec((2, 128, dk), lambda b, h: (0, 0, h)),
        pl.BlockSpec((2, 1, dk), lambda b, h: (0, 0, h)),
        pl.BlockSpec((1, dv), lambda b, h: (0, h)),
        pl.BlockSpec(dsel.shape, lambda b, h: (0, 0)),
    ]
    args = [proj, proj, proj, proj, w2pad, b_a, out_g, dsel]
    if has_state:
        in_specs.append(pl.BlockSpec((1, 1, 2, 1, dk, dv), lambda b, h: (b, 0, 0, h, 0, 0)))
        args.append(state)
    out_specs = [pl.BlockSpec((seq_len, dv), lambda b, h: (b, h))]
    out_shape = [jax.ShapeDtypeStruct((m, dv_total), F32)]
    if emit_state:
        out_specs.append(pl.BlockSpec((1, 1, 2, 1, dk, dv), lambda b, h: (b, 0, 0, h, 0, 0)))
        out_shape.append(jax.ShapeDtypeStruct((n_seq, 1, 2, n_heads, dk, dv), F32))
    return pl.pallas_call(
        functools.partial(_gla_scan_kernel, n_chunks=n_chunks, has_state=has_state, emit_state=emit_state),
        grid=(n_seq, n_heads),
        in_specs=in_specs,
        out_specs=out_specs,
        out_shape=out_shape,
        scratch_shapes=[pltpu.VMEM((dv, dk), F32)],
        compiler_params=_params(("parallel", "parallel")),
        name="gla_scan",
    )(*args)


def _pair_rmsnorm(x, g, lane_lo):
    x2 = x * x
    s_lo = jnp.sum(jnp.where(lane_lo, x2, 0.0), axis=-1, keepdims=True)
    s_hi = jnp.sum(jnp.where(lane_lo, 0.0, x2), axis=-1, keepdims=True)
    ms = jnp.where(lane_lo, s_lo, s_hi) * (2.0 / x.shape[-1])
    return x * lax.rsqrt(ms + EPS) * g


def _na_ctx_kernel(q_ref, k_ref, v_ref, qg_ref, kg_ref, o_ref, kn_ref):
    t, w = q_ref.shape
    dh = w // 2
    scale = dh ** -0.5
    lane_lo = lax.broadcasted_iota(jnp.int32, (1, w), 1) < dh
    qn = _pair_rmsnorm(q_ref[...], qg_ref[...], lane_lo)
    kn = _pair_rmsnorm(k_ref[...], kg_ref[...], lane_lo)
    kn_ref[...] = kn
    kb = kn.astype(BF16)
    vb = v_ref[...].astype(BF16)
    outs = []
    for lo in (True, False):
        sel = lane_lo if lo else jnp.logical_not(lane_lo)
        qh = jnp.where(sel, qn, 0.0).astype(BF16)
        s = _dot_nt(qh, kb) * scale
        p = jnp.exp(s - jnp.max(s, axis=-1, keepdims=True))
        l = jnp.sum(p, axis=-1, keepdims=True)
        outs.append(_dot(p.astype(BF16), vb) / l)
    o_ref[...] = jnp.where(lane_lo, outs[0], outs[1])


def _na_context(proj, q_g2, k_g2, n_seq, seq_len, d_model):
    m = proj.shape[0]
    nb = d_model // 128
    return pl.pallas_call(
        _na_ctx_kernel,
        grid=(n_seq, nb),
        in_specs=[
            pl.BlockSpec((seq_len, 128), lambda b, p: (b, p)),
            pl.BlockSpec((seq_len, 128), lambda b, p: (b, nb + p)),
            pl.BlockSpec((seq_len, 128), lambda b, p: (b, 2 * nb + p)),
            pl.BlockSpec((1, 128), lambda b, p: (0, 0)),
            pl.BlockSpec((1, 128), lambda b, p: (0, 0)),
        ],
        out_specs=[
            pl.BlockSpec((seq_len, 128), lambda b, p: (b, p)),
            pl.BlockSpec((seq_len, 128), lambda b, p: (b, p)),
        ],
        out_shape=[jax.ShapeDtypeStruct((m, d_model), F32), jax.ShapeDtypeStruct((m, d_model), F32)],
        compiler_params=_params(("parallel", "parallel")),
        name="na_context",
    )(proj, proj, proj, q_g2, k_g2)


def _na_latent_kernel(q_ref, k_ref, v_ref, kc_ref, vc_ref, bias_ref, qg_ref, kg_ref, o_ref, qn_scr, kn_scr, vb_scr,
                      *, n_rows, win_h):
    t, w = q_ref.shape
    dh = w // 2
    scale = dh ** -0.5
    gw = t // n_rows
    lane_lo = lax.broadcasted_iota(jnp.int32, (1, w), 1) < dh
    qn_scr[...] = _pair_rmsnorm(q_ref[...], qg_ref[...], lane_lo)
    kn_scr[...] = _pair_rmsnorm(k_ref[...], kg_ref[...], lane_lo).astype(BF16)
    vb_scr[...] = v_ref[...].astype(BF16)
    kc = kc_ref[0].astype(BF16)
    vc = vc_ref[0].astype(BF16)

    def body(r, carry):
        r0 = jnp.clip(r - win_h // 2, 0, n_rows - win_h)
        qr = qn_scr[pl.ds(pl.multiple_of(r * gw, gw), gw), :]
        krows = pl.ds(pl.multiple_of(r0 * gw, gw), win_h * gw)
        kl = kn_scr[krows, :]
        vl = vb_scr[krows, :]
        outs = []
        for hh in range(2):
            sel = lane_lo if hh == 0 else jnp.logical_not(lane_lo)
            qh = jnp.where(sel, qr, 0.0).astype(BF16)
            s_loc = _dot_nt(qh, kl) * scale + bias_ref[hh, r - r0]
            s_ctx = _dot_nt(qh, kc) * scale
            mx = jnp.maximum(jnp.max(s_loc, axis=-1, keepdims=True), jnp.max(s_ctx, axis=-1, keepdims=True))
            p_loc = jnp.exp(s_loc - mx)
            p_ctx = jnp.exp(s_ctx - mx)
            l = jnp.sum(p_loc, axis=-1, keepdims=True) + jnp.sum(p_ctx, axis=-1, keepdims=True)
            outs.append((_dot(p_loc.astype(BF16), vl) + _dot(p_ctx.astype(BF16), vc)) / l)
        o_ref[pl.ds(pl.multiple_of(r * gw, gw), gw), :] = jnp.where(lane_lo, outs[0], outs[1])
        return carry

    lax.fori_loop(0, n_rows, body, 0)


def _na_latent(proj, kc, vc, bias, q_g2, k_g2, n_seq, seq_len, d_model):
    m = proj.shape[0]
    nb = d_model // 128
    n_rows = seq_len // GRID_W
    win_h = min(NA_WIN_H, n_rows)
    past = kc.shape[1]
    return pl.pallas_call(
        functools.partial(_na_latent_kernel, n_rows=n_rows, win_h=win_h),
        grid=(n_seq, nb),
        in_specs=[
            pl.BlockSpec((seq_len, 128), lambda b, p: (b, p)),
            pl.BlockSpec((seq_len, 128), lambda b, p: (b, nb + p)),
            pl.BlockSpec((seq_len, 128), lambda b, p: (b, 2 * nb + p)),
            pl.BlockSpec((1, past, 128), lambda b, p: (b, 0, p)),
            pl.BlockSpec((1, past, 128), lambda b, p: (b, 0, p)),
            pl.BlockSpec((2,) + bias.shape[1:], lambda b, p: (p, 0, 0, 0)),
            pl.BlockSpec((1, 128), lambda b, p: (0, 0)),
            pl.BlockSpec((1, 128), lambda b, p: (0, 0)),
        ],
        out_specs=pl.BlockSpec((seq_len, 128), lambda b, p: (b, p)),
        out_shape=jax.ShapeDtypeStruct((m, d_model), F32),
        scratch_shapes=[pltpu.VMEM((seq_len, 128), F32), pltpu.VMEM((seq_len, 128), BF16),
                        pltpu.VMEM((seq_len, 128), BF16)],
        compiler_params=_params(("parallel", "parallel")),
        name="na_latent",
    )(proj, proj, proj, kc, vc, bias, q_g2, k_g2)


def _na_bias_table(rpb, n_rows):
    win_h = min(NA_WIN_H, n_rows)
    cols = jnp.arange(GRID_W)
    col_start = jnp.clip(cols - NA_WIN_W // 2, 0, GRID_W - NA_WIN_W)
    kc = jnp.arange(GRID_W)
    in_win = (kc[None, :] >= col_start[:, None]) & (kc[None, :] < col_start[:, None] + NA_WIN_W)
    col_rel = jnp.clip(kc[None, :] - cols[:, None] + (NA_WIN_W - 1), 0, 2 * NA_WIN_W - 2)
    off = jnp.arange(win_h)
    kr = jnp.arange(win_h)
    row_rel = kr[None, :] - off[:, None] + (NA_WIN_H - 1)
    t = rpb[:, row_rel][:, :, :, col_rel]
    t = jnp.where(in_win[None, None, None], t, -jnp.inf)
    t = t.transpose(0, 1, 3, 2, 4)
    return t.reshape(rpb.shape[0], win_h, GRID_W, win_h * GRID_W).astype(F32)


def _block_diag_tiles(w, tile):
    n_blk, s, _ = w.shape
    per = tile // s
    w4 = w.reshape(n_blk // per, per, s, s)
    eye = jnp.eye(per, dtype=w.dtype)
    return jnp.einsum('tpio,pq->tpiqo', w4, eye).reshape(n_blk // per, tile, tile)


def _mlstm_layer(streams, mods, norm_g, p, state, j):
    (w_in, conv_w, conv_b, wq, wk, wv, w_ig, b_ig, w_fg, b_fg, out_g, skip, w_out) = p
    inner = conv_w.shape[1]
    n_heads = b_ig.shape[1]
    w_in_b = w_in.astype(BF16)
    w_out_b = w_out.astype(BF16)
    bdq = _block_diag_tiles(wq, 256).astype(BF16)
    bdk = _block_diag_tiles(wk, 256).astype(BF16)
    bdv = _block_diag_tiles(wv, 256).astype(BF16)
    n_g = 4 * n_heads
    wg = jnp.concatenate([w_ig[0], w_ig[1], w_fg[0], w_fg[1]], axis=1)
    wg = jnp.pad(wg, ((0, 0), (0, 128 - n_g))).astype(BF16)
    bg = jnp.pad(jnp.concatenate([b_ig[0], b_ig[1], b_fg[0], b_fg[1]]), (0, 128 - n_g)).reshape(1, 128)
    outs, new_state = [], None
    for (y, n_seq, seq_len, is_prompt), mod in zip(streams, mods):
        xz = _inproj(y, mod, norm_g, w_in_b, seq_len)
        q, k, v, xc, gc, gr = _mlstm_pre(xz, conv_w, conv_b.reshape(1, inner), bdq, bdk, bdv, wg, bg, seq_len, n_heads)
        m = y.shape[0]
        gch = gc.reshape(m, 4, n_heads).transpose(2, 0, 1)
        grh = gr.reshape(4, n_heads, m).transpose(1, 0, 2)
        st = None
        if not is_prompt:
            c0, n0, m0 = state
            st = (c0, n0.reshape(n0.shape[:4] + (1, n0.shape[4])),
                  jnp.broadcast_to(m0[..., None, None], m0.shape + (1, 128)))
        res = _mlstm_scan(q, k, v, xc, gch, grh, out_g.reshape(1, inner), skip.reshape(1, inner), st, j,
                          n_seq, seq_len, n_heads, emit_state=is_prompt)
        if is_prompt:
            u, cn, nn, mn = res
            new_state = (cn, nn[:, :, :, 0, :], mn[:, :, :, 0, 0])
        else:
            u = res[0]
        outs.append(_outproj(u, xz, 1, mod, w_out_b, y, seq_len))
    return outs, new_state


def _gla_layer(streams, mods, norm_g, p, state):
    (w_in, w_a1, w_a2, b_a, out_g, w_out) = p
    d = w_in.shape[0]
    n_heads = state.shape[3]
    dk_total = w_a2.shape[2]
    dv_total = out_g.shape[0]
    pad = 128 - 2 * GLA_RANK
    w_ext = jnp.concatenate([w_in, w_a1[0], w_a1[1], jnp.zeros((d, pad), F32)], axis=1).astype(BF16)
    w2pad = jnp.zeros((2, 128, dk_total), F32)
    w2pad = w2pad.at[0, 0:GLA_RANK].set(w_a2[0]).at[1, GLA_RANK:2 * GLA_RANK].set(w_a2[1]).astype(BF16)
    w_out_b = w_out.astype(BF16)
    L, DB = GLA_CHUNK, GLA_DIAG
    dk = dk_total // n_heads
    dsel = (jnp.arange(DB * dk)[:, None] // dk == jnp.arange(L)[None, :] % DB).astype(BF16)
    outs, new_state = [], None
    for (y, n_seq, seq_len, is_prompt), mod in zip(streams, mods):
        proj = _inproj(y, mod, norm_g, w_ext, seq_len)
        res = _gla_scan(proj, w2pad, b_a.reshape(2, 1, dk_total), out_g.reshape(1, dv_total), dsel,
                        None if is_prompt else state, n_seq, seq_len, n_heads, dk_total, dv_total,
                        emit_state=is_prompt)
        if is_prompt:
            new_state = res[1]
        outs.append(_outproj(res[0], proj, (2 * dk_total + dv_total) // dv_total, mod, w_out_b, y, seq_len))
    return outs, new_state


def _na_layer(streams, mods, norm_g, p, cache_k, cache_v):
    (w_in, q_g, k_g, rpb, w_out) = p
    d = w_in.shape[0]
    n_heads = rpb.shape[0]
    dh = d // n_heads
    w_in_b = w_in.astype(BF16)
    w_out_b = w_out.astype(BF16)
    q_g2 = jnp.tile(q_g, 2).reshape(1, 2 * dh)
    k_g2 = jnp.tile(k_g, 2).reshape(1, 2 * dh)
    outs, new_kv = [], None
    for (y, n_seq, seq_len, is_prompt), mod in zip(streams, mods):
        proj = _inproj(y, mod, norm_g, w_in_b, seq_len)
        if is_prompt:
            o, kn = _na_context(proj, q_g2, k_g2, n_seq, seq_len, d)
            vv = proj[:, 2 * d:3 * d]
            to_heads = lambda a: a.reshape(n_seq, seq_len, n_heads, dh).transpose(0, 2, 1, 3)
            new_kv = (to_heads(kn), to_heads(vv))
        else:
            past = cache_k.shape[2]
            kc = cache_k.transpose(0, 2, 1, 3).reshape(n_seq, past, d)
            vc = cache_v.transpose(0, 2, 1, 3).reshape(n_seq, past, d)
            bias = _na_bias_table(rpb, seq_len // GRID_W)
            o = _na_latent(proj, kc, vc, bias, q_g2, k_g2, n_seq, seq_len, d)
        outs.append(_outproj(o, proj, 3, mod, w_out_b, y, seq_len))
    return outs, new_kv


def kernel(x_prompt, x_sample, state_mlstm_C, state_mlstm_n, state_mlstm_m, state_gla_S, cache_na_k, cache_na_v, c, c_ctx, norm_g, w_mod, b_mod, a_w_in, a_conv_w, a_conv_b, a_wq, a_wk, a_wv, a_w_ig, a_b_ig, a_w_fg, a_b_fg, a_out_g, a_skip, a_w_out, b_w_in, b_w_a1, b_w_a2, b_b_a, b_out_g, b_w_out, c_w_in, c_q_g, c_k_g, c_rpb, c_w_out):
    bp, seq, d = x_prompt.shape
    bs, dec_seq, _ = x_sample.shape
    depth = w_mod.shape[0]

    cs = jnp.zeros((16, d), F32).at[0:bs].set(c).at[bs].set(c_ctx)
    mod_all = _modulation(cs, w_mod, b_mod)

    yp = x_prompt.reshape(bp * seq, d)
    ys = x_sample.reshape(bs * dec_seq, d)
    new_C, new_n, new_m, new_S, new_k, new_v = [], [], [], [], [], []
    for l in range(depth):
        kind, j = l % 3, l // 3
        mod_p = mod_all[l, bs:bs + 1].reshape(1, 1, 3 * d)
        mod_s = mod_all[l, 0:bs].reshape(bs, 1, 3 * d)
        streams = [(yp, bp, seq, True), (ys, bs, dec_seq, False)]
        mods = [mod_p, mod_s]
        g = norm_g[l].reshape(1, d)
        if kind == 0:
            pa = (a_w_in[j], a_conv_w[j], a_conv_b[j], a_wq[j], a_wk[j], a_wv[j], a_w_ig[j], a_b_ig[j],
                  a_w_fg[j], a_b_fg[j], a_out_g[j], a_skip[j], a_w_out[j])
            (yp, ys), (cn, nn, mn) = _mlstm_layer(streams, mods, g, pa,
                                                  (state_mlstm_C, state_mlstm_n, state_mlstm_m), j)
            new_C.append(cn)
            new_n.append(nn)
            new_m.append(mn)
        elif kind == 1:
            pb = (b_w_in[j], b_w_a1[j], b_w_a2[j], b_b_a[j], b_out_g[j], b_w_out[j])
            (yp, ys), sn = _gla_layer(streams, mods, g, pb, state_gla_S[:, j:j + 1])
            new_S.append(sn)
        else:
            pc = (c_w_in[j], c_q_g[j], c_k_g[j], c_rpb[j], c_w_out[j])
            (yp, ys), (kn, vn) = _na_layer(streams, mods, g, pc, cache_na_k[:, j], cache_na_v[:, j])
            new_k.append(kn)
            new_v.append(vn)
    return (yp.reshape(bp, seq, d), ys.reshape(bs, dec_seq, d), jnp.stack(new_C, axis=1), jnp.stack(new_n, axis=1),
            jnp.stack(new_m, axis=1), jnp.concatenate(new_S, axis=1), jnp.stack(new_k, axis=1),
            jnp.stack(new_v, axis=1))
```

```python
import functools

import jax
import jax.numpy as jnp
from jax import lax
from jax.experimental import pallas as pl
from jax.experimental.pallas import tpu as pltpu

F32 = jnp.float32
BF16 = jnp.bfloat16
EPS = 1e-6

GRID_W = 64
NA_WIN_H = 8
NA_WIN_W = 16
GLA_TAU = 16.0
GLA_RANK = 16
MLSTM_CHUNK = 256
GLA_CHUNK = 64
GLA_DIAG = 8
GLA_HEADS_PER_STEP = 4
LOG2E = 1.4426950408889634
ROW_TILE = 512
V7X_VMEM_LIMIT = 56 * 1024 * 1024


def _params(sem):
    return pltpu.CompilerParams(dimension_semantics=sem, vmem_limit_bytes=V7X_VMEM_LIMIT)


def _dot(a, b):
    return jnp.dot(a, b, preferred_element_type=F32)


def _dot_nt(a, b):
    return lax.dot_general(a, b, (((1,), (1,)), ((), ())), preferred_element_type=F32)


def _dot_tn(a, b):
    return lax.dot_general(a, b, (((0,), (0,)), ((), ())), preferred_element_type=F32)


def _split3(x):
    hi = x.astype(BF16)
    r1 = x - hi.astype(F32)
    mid = r1.astype(BF16)
    lo = (r1 - mid.astype(F32)).astype(BF16)
    return hi, mid, lo


def _dot_exact_lhs(sel, x):
    hi, mid, lo = _split3(x)
    return _dot(sel, hi) + (_dot(sel, mid) + _dot(sel, lo))


def _silu(x):
    return x * jax.nn.sigmoid(x)


def _log_sigmoid(x):
    return jnp.minimum(x, 0.0) - jnp.log(1.0 + jnp.exp(-jnp.abs(x)))


def _mod_kernel(c_ref, w_ref, b_ref, o_ref):
    a = _silu(c_ref[...])
    w = w_ref[0]
    ah = a.astype(BF16)
    al = (a - ah.astype(F32)).astype(BF16)
    wh = w.astype(BF16)
    wl = (w - wh.astype(F32)).astype(BF16)
    o_ref[0] = _dot(ah, wh) + (_dot(al, wh) + _dot(ah, wl)) + b_ref[0]


def _modulation(cs, w_mod, b_mod):
    depth, d, d3 = w_mod.shape
    rows = cs.shape[0]
    return pl.pallas_call(
        _mod_kernel,
        grid=(depth, d3 // d),
        in_specs=[
            pl.BlockSpec((rows, d), lambda l, j: (0, 0)),
            pl.BlockSpec((1, d, d), lambda l, j: (l, 0, j)),
            pl.BlockSpec((1, 1, d), lambda l, j: (l, 0, j)),
        ],
        out_specs=pl.BlockSpec((1, rows, d), lambda l, j: (l, 0, j)),
        out_shape=jax.ShapeDtypeStruct((depth, rows, d3), F32),
        compiler_params=_params(("parallel", "parallel")),
        name="modulation",
    )(cs, w_mod, b_mod.reshape(depth, 1, d3))


def _inproj_kernel(x_ref, mod_ref, g_ref, w_ref, o_ref, *tail_ref, d):
    x = x_ref[...]
    y = x * lax.rsqrt(jnp.mean(x * x, axis=-1, keepdims=True) + EPS) * g_ref[...]
    shift = mod_ref[0, :, 0:d]
    scale = mod_ref[0, :, d:2 * d]
    h = y * (1.0 + scale) + shift
    res = _dot(h.astype(BF16), w_ref[...])
    n_main = o_ref.shape[1]
    o_ref[...] = res[:, 0:n_main].astype(o_ref.dtype)
    if tail_ref:
        tail_ref[0][...] = res[:, n_main:]


def _inproj(x, mod, g, w, seq_len, out_dtype=BF16, n_tail=0):
    m, d = x.shape
    n = w.shape[1]
    n_main = n - n_tail
    tm = min(ROW_TILE, seq_len)
    tiles_per_seq = seq_len // tm
    out_specs = [pl.BlockSpec((tm, n_main), lambda i: (i, 0))]
    out_shape = [jax.ShapeDtypeStruct((m, n_main), out_dtype)]
    if n_tail:
        out_specs.append(pl.BlockSpec((tm, n_tail), lambda i: (i, 0)))
        out_shape.append(jax.ShapeDtypeStruct((m, n_tail), F32))
    if mod.shape[0] == 1:
        mod_map = lambda i: (0, 0, 0)
    else:
        mod_map = lambda i: (i // tiles_per_seq, 0, 0)
    return pl.pallas_call(
        functools.partial(_inproj_kernel, d=d),
        grid=(m // tm,),
        in_specs=[
            pl.BlockSpec((tm, d), lambda i: (i, 0)),
            pl.BlockSpec((1, 1, 3 * d), mod_map),
            pl.BlockSpec((1, d), lambda i: (0, 0)),
            pl.BlockSpec((d, n), lambda i: (0, 0)),
        ],
        out_specs=out_specs,
        out_shape=out_shape,
        compiler_params=_params(("parallel",)),
        name="inproj",
    )(x, mod, g, w)


def _outproj_kernel(u_ref, r_ref, mod_ref, w_ref, y_ref, o_ref, *, d):
    a = u_ref[...] * _silu(r_ref[...].astype(F32))
    out = _dot(a.astype(BF16), w_ref[...])
    gate = mod_ref[0, :, 2 * d:3 * d]
    o_ref[...] = y_ref[...] + gate * out


def _outproj(u, r_arr, r_col_block, mod, w, y, seq_len):
    m, kdim = u.shape
    d = y.shape[1]
    tm = min(ROW_TILE, seq_len)
    tiles_per_seq = seq_len // tm
    if mod.shape[0] == 1:
        mod_map = lambda i: (0, 0, 0)
    else:
        mod_map = lambda i: (i // tiles_per_seq, 0, 0)
    return pl.pallas_call(
        functools.partial(_outproj_kernel, d=d),
        grid=(m // tm,),
        in_specs=[
            pl.BlockSpec((tm, kdim), lambda i: (i, 0)),
            pl.BlockSpec((tm, kdim), lambda i: (i, r_col_block)),
            pl.BlockSpec((1, 1, 3 * d), mod_map),
            pl.BlockSpec((kdim, d), lambda i: (0, 0)),
            pl.BlockSpec((tm, d), lambda i: (i, 0)),
        ],
        out_specs=pl.BlockSpec((tm, d), lambda i: (i, 0)),
        out_shape=jax.ShapeDtypeStruct((m, d), F32),
        compiler_params=_params(("parallel",)),
        name="outproj",
    )(u, r_arr, mod, w, y)


def _mlstm_pre_kernel(xm_ref, prev_ref, next_ref, cw_ref, cb_ref, bdq_ref, bdk_ref, bdv_ref, wg_ref, bg_ref,
                      q_ref, k_ref, v_ref, xc_ref, gc_ref, gr_ref, *, tiles_per_seq, n_ig):
    i = pl.program_id(0)
    tm, inner = xm_ref.shape
    xm = xm_ref[...].astype(F32)
    row = lax.broadcasted_iota(jnp.int32, (tm, 1), 0)
    first = (i % tiles_per_seq) == 0
    last = (i % tiles_per_seq) == tiles_per_seq - 1
    halo = prev_ref.shape[0]
    prev_row = jnp.where(first, 0.0, prev_ref[halo - 1:halo, :].astype(F32))
    next_row = jnp.where(last, 0.0, next_ref[0:1, :].astype(F32))
    x_m1 = jnp.where(row == 0, prev_row, pltpu.roll(xm, 1, 0))
    x_p1 = jnp.where(row == tm - 1, next_row, pltpu.roll(xm, tm - 1, 0))
    conv = cw_ref[0:1, :] * x_m1 + cw_ref[1:2, :] * xm + cw_ref[2:3, :] * x_p1 + cb_ref[...]
    xc = _silu(conv)
    xc_ref[...] = xc.astype(xc_ref.dtype)

    xc_b = xc.astype(BF16)
    xm_b = xm.astype(BF16)
    blk = bdq_ref.shape[1]
    acc = jnp.zeros((tm, wg_ref.shape[1]), F32)
    for j in range(inner // blk):
        cols = slice(j * blk, (j + 1) * blk)
        qj = _dot(xc_b[:, cols], bdq_ref[j])
        kj = _dot(xc_b[:, cols], bdk_ref[j])
        vj = _dot(xm_b[:, cols], bdv_ref[j])
        qb, kb, vb = qj.astype(BF16), kj.astype(BF16), vj.astype(BF16)
        q_ref[:, cols] = qb
        k_ref[:, cols] = kb
        v_ref[:, cols] = vb
        acc = acc + _dot(qb, wg_ref[cols, :])
        acc = acc + _dot(kb, wg_ref[inner + j * blk:inner + (j + 1) * blk, :])
        acc = acc + _dot(vb, wg_ref[2 * inner + j * blk:2 * inner + (j + 1) * blk, :])
    g = acc + bg_ref[...]
    lane = lax.broadcasted_iota(jnp.int32, g.shape, 1)
    lf = jnp.where((lane >= n_ig) & (lane < 2 * n_ig), _log_sigmoid(g), 0.0)
    r = lax.broadcasted_iota(jnp.int32, (tm, tm), 0)
    c = lax.broadcasted_iota(jnp.int32, (tm, tm), 1)
    prefix = _dot_exact_lhs((c <= r).astype(BF16), lf)
    suffix = _dot_exact_lhs((c >= r).astype(BF16), lf)
    half = n_ig + n_ig // 2
    gcol = jnp.where(lane < n_ig, g, jnp.where(lane < half, prefix, suffix))
    gc_ref[...] = gcol[:, 0:2 * n_ig]
    gr_ref[...] = jnp.transpose(gcol)[0:2 * n_ig, :]


def _mlstm_pre(xz, conv_w, conv_b, bdq, bdk, bdv, wg, bg, seq_len, n_heads):
    m = xz.shape[0]
    inner = conv_w.shape[1]
    tm = MLSTM_CHUNK
    tiles_per_seq = seq_len // tm
    n_ig = 2 * n_heads
    halo = 16
    last_halo = m // halo - 1
    return pl.pallas_call(
        functools.partial(_mlstm_pre_kernel, tiles_per_seq=tiles_per_seq, n_ig=n_ig),
        grid=(m // tm,),
        in_specs=[
            pl.BlockSpec((tm, inner), lambda i: (i, 0)),
            pl.BlockSpec((halo, inner), lambda i: (jnp.maximum(i * (tm // halo) - 1, 0), 0)),
            pl.BlockSpec((halo, inner), lambda i: (jnp.minimum((i + 1) * (tm // halo), last_halo), 0)),
            pl.BlockSpec(conv_w.shape, lambda i: (0, 0)),
            pl.BlockSpec((1, inner), lambda i: (0, 0)),
            pl.BlockSpec(bdq.shape, lambda i: (0, 0, 0)),
            pl.BlockSpec(bdk.shape, lambda i: (0, 0, 0)),
            pl.BlockSpec(bdv.shape, lambda i: (0, 0, 0)),
            pl.BlockSpec(wg.shape, lambda i: (0, 0)),
            pl.BlockSpec(bg.shape, lambda i: (0, 0)),
        ],
        out_specs=[
            pl.BlockSpec((tm, inner), lambda i: (i, 0)),
            pl.BlockSpec((tm, inner), lambda i: (i, 0)),
            pl.BlockSpec((tm, inner), lambda i: (i, 0)),
            pl.BlockSpec((tm, inner), lambda i: (i, 0)),
            pl.BlockSpec((tm, 2 * n_ig), lambda i: (i, 0)),
            pl.BlockSpec((2 * n_ig, tm), lambda i: (0, i)),
        ],
        out_shape=[
            jax.ShapeDtypeStruct((m, inner), BF16),
            jax.ShapeDtypeStruct((m, inner), BF16),
            jax.ShapeDtypeStruct((m, inner), BF16),
            jax.ShapeDtypeStruct((m, inner), BF16),
            jax.ShapeDtypeStruct((m, 2 * n_ig), F32),
            jax.ShapeDtypeStruct((2 * n_ig, m), F32),
        ],
        compiler_params=_params(("parallel",)),
        name="mlstm_pre",
    )(xz, xz, xz, conv_w, conv_b, bdq, bdk, bdv, wg, bg)


def _mlstm_scan_kernel(*refs, n_chunks, has_state, emit_state, has_alias):
    q_ref, k_ref, v_ref, xc_ref, gc_ref, gr_ref, og_ref, skip_ref = refs[:8]
    pos = 8
    if has_state:
        c0_ref, n0_ref, m0_ref = refs[pos:pos + 3]
        pos += 3
    pos += int(has_alias)
    u_ref = refs[pos]
    pos += 1
    if emit_state:
        cn_ref, nn_ref, mn_ref = refs[pos:pos + 3]
        pos += 3
    c_scr, n_scr, m_scr = refs[pos:pos + 3]

    L = MLSTM_CHUNK
    dh = q_ref.shape[1]
    scale = dh ** -0.5
    row = lax.broadcasted_iota(jnp.int32, (L, L), 0)
    col = lax.broadcasted_iota(jnp.int32, (L, L), 1)

    def gates(rows, d):
        gc = gc_ref[0, rows, :]
        gr = gr_ref[0, :, rows]
        ig_col, b_col = gc[:, d:d + 1], gc[:, 2 + d:3 + d]
        ig_row, b_row = gr[d:d + 1, :], gr[2 + d:3 + d, :]
        mask = (col <= row) if d == 0 else (col >= row)
        dmat = jnp.where(mask, b_col - b_row + ig_row, -jnp.inf)
        b_last = b_col[L - 1:L, :] if d == 0 else b_col[0:1, :]
        return dmat, b_col, b_last - b_col + ig_col, b_last

    def finalize(rows, hs):
        mu = jnp.mean(hs, axis=-1, keepdims=True)
        hc = hs - mu
        hn = hc * lax.rsqrt(jnp.mean(hc * hc, axis=-1, keepdims=True) + EPS) * og_ref[...]
        u_ref[rows, :] = hn + skip_ref[...] * xc_ref[rows, :].astype(F32)

    if n_chunks == 1 and not has_state:
        rows = pl.ds(0, L)
        q, k, v = q_ref[...], k_ref[...], v_ref[...]
        s_qk = _dot_nt(q, k) * scale
        p = None
        for d in range(2):
            dmat, b_col, w_state, b_last = gates(rows, d)
            m_t = jnp.max(dmat, axis=1, keepdims=True)
            s = s_qk * jnp.exp(dmat - m_t)
            den = jnp.sum(s, axis=1, keepdims=True)
            s = s / jnp.maximum(jnp.abs(den), jnp.exp(-m_t))
            p = s if p is None else p + s
            if emit_state:
                m_new = jnp.maximum(b_last, jnp.max(w_state, axis=0, keepdims=True))
                kw = k.astype(F32) * jnp.exp(w_state - m_new)
                cn_ref[0, 0, d, 0] = _dot_tn(kw.astype(BF16), v)
                nn_ref[0, d, 0] = jnp.sum(kw, axis=0, keepdims=True)
                mn_ref[0, d, 0] = jnp.broadcast_to(m_new, mn_ref.shape[3:])
        finalize(rows, _dot(p.astype(BF16), v))
        return

    assert n_chunks % 2 == 0

    def direction(c, d, finish, update):
        rows = pl.ds(pl.multiple_of(c * L, L), L)
        q = q_ref[rows, :]
        k = k_ref[rows, :]
        v = v_ref[rows, :]
        dmat, b_col, w_state, b_last = gates(rows, d)
        m_prev = m_scr[d]
        c_prev = c_scr[d]
        n_prev = n_scr[d]
        inter = b_col + m_prev
        m_t = jnp.maximum(inter, jnp.max(dmat, axis=1, keepdims=True))
        s = _dot_nt(q, k) * (scale * jnp.exp(dmat - m_t))
        w_inter = scale * jnp.exp(inter - m_t)
        num = _dot(s.astype(BF16), v) + w_inter * _dot(q, c_prev.astype(BF16))
        den = (jnp.sum(s, axis=1, keepdims=True)
               + w_inter * jnp.sum(q.astype(F32) * n_prev, axis=1, keepdims=True))
        h = num / jnp.maximum(jnp.abs(den), jnp.exp(-m_t))
        if finish:
            finalize(rows, u_ref[rows, :] + h)
        else:
            u_ref[rows, :] = h
        if update:
            m_new = jnp.maximum(b_last + m_prev, jnp.max(w_state, axis=0, keepdims=True))
            decay = jnp.exp(b_last + m_prev - m_new)
            kw = k.astype(F32) * jnp.exp(w_state - m_new)
            c_scr[d] = decay * c_prev + _dot_tn(kw.astype(BF16), v)
            n_scr[d] = decay * n_prev + jnp.sum(kw, axis=0, keepdims=True)
            m_scr[d] = m_new

    def step(s, finish, update):
        direction(s, 0, finish, update)
        direction(n_chunks - 1 - s, 1, finish, update)

    for d in range(2):
        if has_state:
            c_scr[d] = c0_ref[0, 0, d, 0]
            n_scr[d] = n0_ref[0, 0, d, 0]
            m_scr[d] = m0_ref[0, 0, d, 0][:, 0:1]
        else:
            c_scr[d] = jnp.zeros((dh, dh), F32)
            n_scr[d] = jnp.zeros((1, dh), F32)
            m_scr[d] = jnp.zeros((1, 1), F32)

    half = n_chunks // 2

    def first_half(s, carry):
        step(s, False, True)
        return carry

    def second_half(s, carry):
        step(s, True, True)
        return carry

    lax.fori_loop(0, half, first_half, 0)
    lax.fori_loop(half, n_chunks - 1, second_half, 0)
    step(n_chunks - 1, True, emit_state)
    if emit_state:
        for d in range(2):
            cn_ref[0, 0, d, 0] = c_scr[d]
            nn_ref[0, d, 0] = n_scr[d]
            mn_ref[0, d, 0] = jnp.broadcast_to(m_scr[d], mn_ref.shape[3:])


def _mlstm_scan(q, k, v, xc, gch, grh, out_g, skip, state, layer_idx, n_layers, c_stack, n_seq, seq_len, n_heads,
                emit_state):
    m, inner = q.shape
    dh = inner // n_heads
    n_chunks = seq_len // MLSTM_CHUNK
    has_state = state is not None
    j = layer_idx
    tok = lambda b, h: (b, h)
    in_specs = [
        pl.BlockSpec((seq_len, dh), tok),
        pl.BlockSpec((seq_len, dh), tok),
        pl.BlockSpec((seq_len, dh), tok),
        pl.BlockSpec((seq_len, dh), tok),
        pl.BlockSpec((1, seq_len, 4), lambda b, h: (h, b, 0)),
        pl.BlockSpec((1, 4, seq_len), lambda b, h: (h, 0, b)),
        pl.BlockSpec((1, dh), lambda b, h: (0, h)),
        pl.BlockSpec((1, dh), lambda b, h: (0, h)),
    ]
    args = [q, k, v, xc, gch, grh, out_g, skip]
    if has_state:
        c0, n0, m0 = state
        in_specs += [
            pl.BlockSpec((1, 1, 2, 1, dh, dh), lambda b, h: (b, j, 0, h, 0, 0)),
            pl.BlockSpec((1, 1, 2, 1, 1, dh), lambda b, h: (b, j, 0, h, 0, 0)),
            pl.BlockSpec((1, 1, 2, 1, 1, 128), lambda b, h: (b, j, 0, h, 0, 0)),
        ]
        args += [c0, n0, m0]
    out_specs = [pl.BlockSpec((seq_len, dh), tok)]
    out_shape = [jax.ShapeDtypeStruct((m, inner), F32)]
    aliases = {}
    if emit_state:
        out_specs += [
            pl.BlockSpec((1, 1, 2, 1, dh, dh), lambda b, h: (b, j, 0, h, 0, 0)),
            pl.BlockSpec((1, 2, 1, 1, dh), lambda b, h: (b, 0, h, 0, 0)),
            pl.BlockSpec((1, 2, 1, 1, 128), lambda b, h: (b, 0, h, 0, 0)),
        ]
        out_shape += [
            jax.ShapeDtypeStruct((n_seq, n_layers, 2, n_heads, dh, dh), F32),
            jax.ShapeDtypeStruct((n_seq, 2, n_heads, 1, dh), F32),
            jax.ShapeDtypeStruct((n_seq, 2, n_heads, 1, 128), F32),
        ]
        if c_stack is not None:
            in_specs.append(pl.BlockSpec(memory_space=pl.ANY))
            args.append(c_stack)
            aliases = {len(args) - 1: 1}
    return pl.pallas_call(
        functools.partial(_mlstm_scan_kernel, n_chunks=n_chunks, has_state=has_state, emit_state=emit_state,
                          has_alias=bool(aliases)),
        grid=(n_seq, n_heads),
        in_specs=in_specs,
        out_specs=out_specs,
        out_shape=out_shape,
        input_output_aliases=aliases,
        scratch_shapes=[pltpu.VMEM((2, dh, dh), F32), pltpu.VMEM((2, 1, dh), F32), pltpu.VMEM((2, 1, 1), F32)],
        compiler_params=_params(("parallel", "parallel")),
        name="mlstm_scan",
    )(*args)


def _gla_scan_kernel(*refs, n_chunks, heads, has_state, emit_state):
    q_ref, k_ref, v_ref, a_ref, w2_ref, ba_ref, og_ref, dsel_ref = refs[:8]
    pos = 8
    if has_state:
        s0_ref = refs[pos]
        pos += 1
    o_ref = refs[pos]
    pos += 1
    if emit_state:
        sn_ref = refs[pos]
        pos += 1
    st_scr = refs[pos]

    L = GLA_CHUNK
    DB = GLA_DIAG
    dk = q_ref.shape[1] // heads
    dv = v_ref.shape[1] // heads
    qscale = dk ** -0.5
    row = lax.broadcasted_iota(jnp.int32, (L, L), 0)
    col = lax.broadcasted_iota(jnp.int32, (L, L), 1)
    sub = lax.broadcasted_iota(jnp.int32, (L // DB, DB, 1), 1)
    same_diag = (row // DB) == (col // DB)

    def direction(c, d, finish, update):
        rev = d == 1
        rows = pl.ds(pl.multiple_of(c * L, L), L)
        la = _log_sigmoid(_dot(a_ref[rows, :].astype(BF16), w2_ref[d]) + ba_ref[d]) * (LOG2E / GLA_TAU)
        tri = jnp.where((col >= row) if rev else (col <= row), 1.0, 0.0).astype(BF16)
        b_all = _dot_exact_lhs(tri, la)
        for hh in range(heads):
            kcols = slice(hh * dk, (hh + 1) * dk)
            vcols = slice(hh * dv, (hh + 1) * dv)
            q = q_ref[rows, kcols].astype(F32) * qscale
            k = k_ref[rows, kcols].astype(F32)
            v = v_ref[rows, vcols].astype(BF16)
            b = b_all[:, kcols]
            st = st_scr[d, hh]

            o = _dot_nt((q * jnp.exp2(b)).astype(BF16), st.astype(BF16))

            att = jnp.zeros((L, L), F32)
            blk = L // 2
            while blk >= DB:
                pieces = []
                for g in range(L // (2 * blk)):
                    idx = g * 2 * blk + (blk if rev else blk - 1)
                    pieces.append(jnp.broadcast_to(b[idx:idx + 1, :], (2 * blk, dk)))
                ref_rows = pieces[0] if len(pieces) == 1 else jnp.concatenate(pieces, axis=0)
                e = jnp.exp2(-jnp.abs(b - ref_rows))
                a_lvl = _dot_nt((q * e).astype(BF16), (k * e).astype(BF16))
                rb, cb = row // blk, col // blk
                if rev:
                    pair = ((rb % 2) == 0) & (cb == rb + 1)
                else:
                    pair = ((rb % 2) == 1) & (cb == rb - 1)
                att = jnp.where(pair, a_lvl, att)
                blk //= 2

            q3 = q.reshape(L // DB, DB, dk)
            k3 = k.reshape(L // DB, DB, dk)
            b3 = b.reshape(L // DB, DB, dk)
            parts = []
            for j in range(DB):
                valid = (sub <= j) if rev else (sub >= j)
                x = q3 * k3[:, j:j + 1, :] * jnp.exp2(jnp.where(valid, b3 - b3[:, j:j + 1, :], -jnp.inf))
                parts.append(x.reshape(L, dk).astype(BF16))
            diag = _dot(jnp.concatenate(parts, axis=1), dsel_ref[...])
            att = jnp.where(same_diag, diag, att)

            o = o + _dot(att.astype(BF16), v)
            if not finish:
                o_ref[rows, vcols] = o
            else:
                os_ = o_ref[rows, vcols] + o
                o_ref[rows, vcols] = (os_ * lax.rsqrt(jnp.mean(os_ * os_, axis=-1, keepdims=True) + EPS)
                                      * og_ref[:, vcols])
            if update:
                b_last = b[0:1, :] if rev else b[L - 1:L, :]
                kt = (k * jnp.exp2(b_last - b)).astype(BF16)
                st_scr[d, hh] = st * jnp.exp2(b_last) + _dot_tn(v, kt)

    def step(s, finish, update):
        direction(s, 0, finish, update)
        direction(n_chunks - 1 - s, 1, finish, update)

    for d in range(2):
        for hh in range(heads):
            if has_state:
                st_scr[d, hh] = jnp.transpose(s0_ref[0, 0, d, hh])
            else:
                st_scr[d, hh] = jnp.zeros((dv, dk), F32)

    half = n_chunks // 2

    def first_half(s, carry):
        step(s, False, True)
        return carry

    def second_half(s, carry):
        step(s, True, True)
        return carry

    lax.fori_loop(0, half, first_half, 0)
    lax.fori_loop(half, n_chunks - 1, second_half, 0)
    step(n_chunks - 1, True, emit_state)
    if emit_state:
        for d in range(2):
            for hh in range(heads):
                sn_ref[0, 0, d, hh] = jnp.transpose(st_scr[d, hh])


def _gla_scan(proj, aproj, w2pad, b_a, out_g, dsel, state, n_seq, seq_len, n_heads, dk_total, dv_total, emit_state):
    m = proj.shape[0]
    heads = GLA_HEADS_PER_STEP
    dk = dk_total // n_heads
    dv = dv_total // n_heads
    n_chunks = seq_len // GLA_CHUNK
    assert n_chunks % 2 == 0 and n_heads % heads == 0
    has_state = state is not None
    wk, wv = heads * dk, heads * dv
    in_specs = [
        pl.BlockSpec((seq_len, wk), lambda b, h: (b, h)),
        pl.BlockSpec((seq_len, wk), lambda b, h: (b, dk_total // wk + h)),
        pl.BlockSpec((seq_len, wv), lambda b, h: (b, 2 * dk_total // wv + h)),
        pl.BlockSpec((seq_len, 128), lambda b, h: (b, 0)),
        pl.BlockSpec((2, 128, wk), lambda b, h: (0, 0, h)),
        pl.BlockSpec((2, 1, wk), lambda b, h: (0, 0, h)),
        pl.BlockSpec((1, wv), lambda b, h: (0, h)),
        pl.BlockSpec(dsel.shape, lambda b, h: (0, 0)),
    ]
    args = [proj, proj, proj, aproj, w2pad, b_a, out_g, dsel]
    if has_state:
        in_specs.append(pl.BlockSpec((1, 1, 2, heads, dk, dv), lambda b, h: (b, 0, 0, h, 0, 0)))
        args.append(state)
    out_specs = [pl.BlockSpec((seq_len, wv), lambda b, h: (b, h))]
    out_shape = [jax.ShapeDtypeStruct((m, dv_total), F32)]
    if emit_state:
        out_specs.append(pl.BlockSpec((1, 1, 2, heads, dk, dv), lambda b, h: (b, 0, 0, h, 0, 0)))
        out_shape.append(jax.ShapeDtypeStruct((n_seq, 1, 2, n_heads, dk, dv), F32))
    return pl.pallas_call(
        functools.partial(_gla_scan_kernel, n_chunks=n_chunks, heads=heads, has_state=has_state,
                          emit_state=emit_state),
        grid=(n_seq, n_heads // heads),
        in_specs=in_specs,
        out_specs=out_specs,
        out_shape=out_shape,
        scratch_shapes=[pltpu.VMEM((2, heads, dv, dk), F32)],
        compiler_params=_params(("parallel", "parallel")),
        name="gla_scan",
    )(*args)


def _pair_rmsnorm(x, g, lane_lo):
    w = x.shape[-1]
    r = lax.broadcasted_iota(jnp.int32, (w, w), 0) < w // 2
    c = lax.broadcasted_iota(jnp.int32, (w, w), 1) < w // 2
    same_half = jnp.where(r == c, 1.0, 0.0).astype(BF16)
    x2 = x * x
    hi = x2.astype(BF16)
    lo = (x2 - hi.astype(F32)).astype(BF16)
    ms = (_dot(hi, same_half) + _dot(lo, same_half)) * (2.0 / w)
    return x * lax.rsqrt(ms + EPS) * g


def _na_ctx_kernel(q_ref, k_ref, v_ref, qg_ref, kg_ref, o_ref, kn_ref):
    t, w = q_ref.shape
    dh = w // 2
    scale = dh ** -0.5
    lane_lo = lax.broadcasted_iota(jnp.int32, (1, w), 1) < dh
    qn = _pair_rmsnorm(q_ref[...].astype(F32), qg_ref[...], lane_lo)
    kn = _pair_rmsnorm(k_ref[...].astype(F32), kg_ref[...], lane_lo)
    kn_ref[...] = kn
    kb = kn.astype(BF16)
    vb = v_ref[...].astype(BF16)
    outs = []
    for lo in (True, False):
        sel = lane_lo if lo else jnp.logical_not(lane_lo)
        qh = jnp.where(sel, qn, 0.0).astype(BF16)
        s = _dot_nt(qh, kb) * scale
        p = jnp.exp(s - jnp.max(s, axis=-1, keepdims=True))
        l = jnp.sum(p, axis=-1, keepdims=True)
        outs.append(_dot(p.astype(BF16), vb) / l)
    o_ref[...] = jnp.where(lane_lo, outs[0], outs[1])


def _na_context(proj, q_g2, k_g2, n_seq, seq_len, d_model):
    m = proj.shape[0]
    nb = d_model // 128
    return pl.pallas_call(
        _na_ctx_kernel,
        grid=(n_seq, nb),
        in_specs=[
            pl.BlockSpec((seq_len, 128), lambda b, p: (b, p)),
            pl.BlockSpec((seq_len, 128), lambda b, p: (b, nb + p)),
            pl.BlockSpec((seq_len, 128), lambda b, p: (b, 2 * nb + p)),
            pl.BlockSpec((1, 128), lambda b, p: (0, 0)),
            pl.BlockSpec((1, 128), lambda b, p: (0, 0)),
        ],
        out_specs=[
            pl.BlockSpec((seq_len, 128), lambda b, p: (b, p)),
            pl.BlockSpec((seq_len, 128), lambda b, p: (b, p)),
        ],
        out_shape=[jax.ShapeDtypeStruct((m, d_model), F32), jax.ShapeDtypeStruct((m, d_model), F32)],
        compiler_params=_params(("parallel", "parallel")),
        name="na_context",
    )(proj, proj, proj, q_g2, k_g2)


def _na_latent_kernel(q_ref, k_ref, v_ref, kc_ref, vc_ref, bias_ref, qg_ref, kg_ref, o_ref, qn_scr, kn_scr, vb_scr,
                      kc_scr, vc_scr, s_scr, p_scr, l_scr, *, n_rows, win_h):
    t, w = q_ref.shape
    dh = w // 2
    gw = t // n_rows
    lane_lo = lax.broadcasted_iota(jnp.int32, (1, w), 1) < dh
    qn_scr[...] = _pair_rmsnorm(q_ref[...].astype(F32), qg_ref[...], lane_lo) * (dh ** -0.5 * LOG2E)
    kn_scr[...] = _pair_rmsnorm(k_ref[...].astype(F32), kg_ref[...], lane_lo).astype(BF16)
    vb_scr[...] = v_ref[...].astype(BF16)
    kc_scr[...] = kc_ref[0].astype(BF16)
    vc_scr[...] = vc_ref[0].astype(BF16)
    n_loc = win_h * gw

    def window(r):
        r0 = jnp.clip(r - win_h // 2, 0, n_rows - win_h)
        return r0, pl.ds(pl.multiple_of(r0 * gw, gw), n_loc)

    def scores(r, slot):
        r0, krows = window(r)
        qr = qn_scr[pl.ds(pl.multiple_of(r * gw, gw), gw), :]
        kl = kn_scr[krows, :]
        for hh in range(2):
            sel = lane_lo if hh == 0 else jnp.logical_not(lane_lo)
            qh = jnp.where(sel, qr, 0.0).astype(BF16)
            s_scr[slot, hh, :, 0:n_loc] = _dot_nt(qh, kl) + bias_ref[hh, r - r0]
            s_scr[slot, hh, :, n_loc:] = _dot_nt(qh, kc_scr[...])

    def probabilities(slot):
        for hh in range(2):
            s = s_scr[slot, hh]
            p = jnp.exp2(s - jnp.max(s, axis=-1, keepdims=True))
            l_scr[slot, hh] = jnp.sum(p, axis=-1, keepdims=True)
            p_scr[slot, hh] = p.astype(BF16)

    def output(r, slot):
        _, krows = window(r)
        vl = vb_scr[krows, :]
        outs = []
        for hh in range(2):
            o = _dot(p_scr[slot, hh, :, 0:n_loc], vl) + _dot(p_scr[slot, hh, :, n_loc:], vc_scr[...])
            outs.append(o / l_scr[slot, hh])
        o_ref[pl.ds(pl.multiple_of(r * gw, gw), gw), :] = jnp.where(lane_lo, outs[0], outs[1])

    scores(0, 0)
    scores(1, 1)
    probabilities(0)

    def body(ii, carry):
        for slot in range(2):
            i = 2 * ii + slot
            output(i, slot)
            probabilities(1 - slot)
            scores(i + 2, slot)
        return carry

    lax.fori_loop(0, n_rows // 2 - 1, body, 0)
    output(n_rows - 2, 0)
    probabilities(1)
    output(n_rows - 1, 1)


def _na_latent(proj, kc, vc, bias, q_g2, k_g2, n_seq, seq_len, d_model):
    m = proj.shape[0]
    nb = d_model // 128
    n_rows = seq_len // GRID_W
    win_h = min(NA_WIN_H, n_rows)
    past = kc.shape[1]
    n_keys = win_h * GRID_W + past
    assert n_rows % 2 == 0 and n_rows >= 4
    return pl.pallas_call(
        functools.partial(_na_latent_kernel, n_rows=n_rows, win_h=win_h),
        grid=(n_seq, nb),
        in_specs=[
            pl.BlockSpec((seq_len, 128), lambda b, p: (b, p)),
            pl.BlockSpec((seq_len, 128), lambda b, p: (b, nb + p)),
            pl.BlockSpec((seq_len, 128), lambda b, p: (b, 2 * nb + p)),
            pl.BlockSpec((1, past, 128), lambda b, p: (b, 0, p)),
            pl.BlockSpec((1, past, 128), lambda b, p: (b, 0, p)),
            pl.BlockSpec((2,) + bias.shape[1:], lambda b, p: (p, 0, 0, 0)),
            pl.BlockSpec((1, 128), lambda b, p: (0, 0)),
            pl.BlockSpec((1, 128), lambda b, p: (0, 0)),
        ],
        out_specs=pl.BlockSpec((seq_len, 128), lambda b, p: (b, p)),
        out_shape=jax.ShapeDtypeStruct((m, d_model), F32),
        scratch_shapes=[pltpu.VMEM((seq_len, 128), F32), pltpu.VMEM((seq_len, 128), BF16),
                        pltpu.VMEM((seq_len, 128), BF16), pltpu.VMEM((past, 128), BF16),
                        pltpu.VMEM((past, 128), BF16), pltpu.VMEM((2, 2, GRID_W, n_keys), F32),
                        pltpu.VMEM((2, 2, GRID_W, n_keys), BF16), pltpu.VMEM((2, 2, GRID_W, 1), F32)],
        compiler_params=_params(("parallel", "parallel")),
        name="na_latent",
    )(proj, proj, proj, kc, vc, bias, q_g2, k_g2)


def _na_bias_table(rpb, n_rows):
    win_h = min(NA_WIN_H, n_rows)
    cols = jnp.arange(GRID_W)
    col_start = jnp.clip(cols - NA_WIN_W // 2, 0, GRID_W - NA_WIN_W)
    kc = jnp.arange(GRID_W)
    in_win = (kc[None, :] >= col_start[:, None]) & (kc[None, :] < col_start[:, None] + NA_WIN_W)
    col_rel = jnp.clip(kc[None, :] - cols[:, None] + (NA_WIN_W - 1), 0, 2 * NA_WIN_W - 2)
    off = jnp.arange(win_h)
    kr = jnp.arange(win_h)
    row_rel = kr[None, :] - off[:, None] + (NA_WIN_H - 1)
    t = rpb[:, row_rel][:, :, :, col_rel]
    t = jnp.where(in_win[None, None, None], t * LOG2E, -jnp.inf)
    t = t.transpose(0, 1, 3, 2, 4)
    return t.reshape(rpb.shape[0], win_h, GRID_W, win_h * GRID_W).astype(F32)


def _block_diag_tiles(w, tile):
    n_blk, s, _ = w.shape
    per = tile // s
    w4 = w.reshape(n_blk // per, per, s, s)
    eye = jnp.eye(per, dtype=w.dtype)
    return jnp.einsum('tpio,pq->tpiqo', w4, eye).reshape(n_blk // per, tile, tile)


def _mlstm_layer(streams, mods, norm_g, p, state, j, n_layers, c_stack):
    (w_in, conv_w, conv_b, wq, wk, wv, w_ig, b_ig, w_fg, b_fg, out_g, skip, w_out) = p
    inner = conv_w.shape[1]
    n_heads = b_ig.shape[1]
    w_in_b = w_in.astype(BF16)
    w_out_b = w_out.astype(BF16)
    bdq = _block_diag_tiles(wq, 256).astype(BF16)
    bdk = _block_diag_tiles(wk, 256).astype(BF16)
    bdv = _block_diag_tiles(wv, 256).astype(BF16)
    n_g = 4 * n_heads
    wg = jnp.concatenate([w_ig[0], w_ig[1], w_fg[0], w_fg[1]], axis=1)
    wg = jnp.pad(wg, ((0, 0), (0, 128 - n_g))).astype(BF16)
    bg = jnp.pad(jnp.concatenate([b_ig[0], b_ig[1], b_fg[0], b_fg[1]]), (0, 128 - n_g)).reshape(1, 128)
    outs, new_state = [], None
    for (y, n_seq, seq_len, is_prompt), mod in zip(streams, mods):
        xz = _inproj(y, mod, norm_g, w_in_b, seq_len)[0]
        q, k, v, xc, gc, gr = _mlstm_pre(xz, conv_w, conv_b.reshape(1, inner), bdq, bdk, bdv, wg, bg, seq_len, n_heads)
        m = y.shape[0]
        gch = gc.reshape(m, 4, n_heads).transpose(2, 0, 1)
        grh = gr.reshape(4, n_heads, m).transpose(1, 0, 2)
        st = None
        if not is_prompt:
            c0, n0, m0 = state
            st = (c0, n0.reshape(n0.shape[:4] + (1, n0.shape[4])),
                  jnp.broadcast_to(m0[..., None, None], m0.shape + (1, 128)))
        res = _mlstm_scan(q, k, v, xc, gch, grh, out_g.reshape(1, inner), skip.reshape(1, inner), st, j, n_layers,
                          c_stack, n_seq, seq_len, n_heads, emit_state=is_prompt)
        if is_prompt:
            u, cn, nn, mn = res
            new_state = (cn, nn[:, :, :, 0, :], mn[:, :, :, 0, 0])
        else:
            u = res[0]
        outs.append(_outproj(u, xz, 1, mod, w_out_b, y, seq_len))
    return outs, new_state


def _gla_layer(streams, mods, norm_g, p, state):
    (w_in, w_a1, w_a2, b_a, out_g, w_out) = p
    d = w_in.shape[0]
    n_heads = state.shape[3]
    dk_total = w_a2.shape[2]
    dv_total = out_g.shape[0]
    pad = 128 - 2 * GLA_RANK
    w_ext = jnp.concatenate([w_in, w_a1[0], w_a1[1], jnp.zeros((d, pad), F32)], axis=1).astype(BF16)
    w2pad = jnp.zeros((2, 128, dk_total), F32)
    w2pad = w2pad.at[0, 0:GLA_RANK].set(w_a2[0]).at[1, GLA_RANK:2 * GLA_RANK].set(w_a2[1]).astype(BF16)
    w_out_b = w_out.astype(BF16)
    L, DB = GLA_CHUNK, GLA_DIAG
    dk = dk_total // n_heads
    dsel = (jnp.arange(DB * dk)[:, None] // dk == jnp.arange(L)[None, :] % DB).astype(BF16)
    outs, new_state = [], None
    for (y, n_seq, seq_len, is_prompt), mod in zip(streams, mods):
        proj, aproj = _inproj(y, mod, norm_g, w_ext, seq_len, n_tail=128)
        res = _gla_scan(proj, aproj, w2pad, b_a.reshape(2, 1, dk_total), out_g.reshape(1, dv_total), dsel,
                        None if is_prompt else state, n_seq, seq_len, n_heads, dk_total, dv_total,
                        emit_state=is_prompt)
        if is_prompt:
            new_state = res[1]
        outs.append(_outproj(res[0], proj, (2 * dk_total + dv_total) // dv_total, mod, w_out_b, y, seq_len))
    return outs, new_state


def _na_layer(streams, mods, norm_g, p, cache_k, cache_v):
    (w_in, q_g, k_g, rpb, w_out) = p
    d = w_in.shape[0]
    n_heads = rpb.shape[0]
    dh = d // n_heads
    w_in_b = w_in.astype(BF16)
    w_out_b = w_out.astype(BF16)
    q_g2 = jnp.tile(q_g, 2).reshape(1, 2 * dh)
    k_g2 = jnp.tile(k_g, 2).reshape(1, 2 * dh)
    outs, new_kv = [], None
    for (y, n_seq, seq_len, is_prompt), mod in zip(streams, mods):
        proj = _inproj(y, mod, norm_g, w_in_b, seq_len, out_dtype=F32 if is_prompt else BF16)[0]
        if is_prompt:
            o, kn = _na_context(proj, q_g2, k_g2, n_seq, seq_len, d)
            vv = proj[:, 2 * d:3 * d]
            to_heads = lambda a: a.reshape(n_seq, seq_len, n_heads, dh).transpose(0, 2, 1, 3)
            new_kv = (to_heads(kn), to_heads(vv))
        else:
            past = cache_k.shape[2]
            kc = cache_k.transpose(0, 2, 1, 3).reshape(n_seq, past, d)
            vc = cache_v.transpose(0, 2, 1, 3).reshape(n_seq, past, d)
            bias = _na_bias_table(rpb, seq_len // GRID_W)
            o = _na_latent(proj, kc, vc, bias, q_g2, k_g2, n_seq, seq_len, d)
        outs.append(_outproj(o, proj, 3, mod, w_out_b, y, seq_len))
    return outs, new_kv


def kernel(x_prompt, x_sample, state_mlstm_C, state_mlstm_n, state_mlstm_m, state_gla_S, cache_na_k, cache_na_v, c, c_ctx, norm_g, w_mod, b_mod, a_w_in, a_conv_w, a_conv_b, a_wq, a_wk, a_wv, a_w_ig, a_b_ig, a_w_fg, a_b_fg, a_out_g, a_skip, a_w_out, b_w_in, b_w_a1, b_w_a2, b_b_a, b_out_g, b_w_out, c_w_in, c_q_g, c_k_g, c_rpb, c_w_out):
    bp, seq, d = x_prompt.shape
    bs, dec_seq, _ = x_sample.shape
    depth = w_mod.shape[0]
    n_mlstm = state_mlstm_C.shape[1]

    cs = jnp.zeros((16, d), F32).at[0:bs].set(c).at[bs].set(c_ctx)
    mod_all = _modulation(cs, w_mod, b_mod)

    yp = x_prompt.reshape(bp * seq, d)
    ys = x_sample.reshape(bs * dec_seq, d)
    c_stack = None
    new_n, new_m, new_S, new_k, new_v = [], [], [], [], []
    for l in range(depth):
        kind, j = l % 3, l // 3
        mod_p = mod_all[l, bs:bs + 1].reshape(1, 1, 3 * d)
        mod_s = mod_all[l, 0:bs].reshape(bs, 1, 3 * d)
        streams = [(yp, bp, seq, True), (ys, bs, dec_seq, False)]
        mods = [mod_p, mod_s]
        g = norm_g[l].reshape(1, d)
        if kind == 0:
            pa = (a_w_in[j], a_conv_w[j], a_conv_b[j], a_wq[j], a_wk[j], a_wv[j], a_w_ig[j], a_b_ig[j],
                  a_w_fg[j], a_b_fg[j], a_out_g[j], a_skip[j], a_w_out[j])
            (yp, ys), (c_stack, nn, mn) = _mlstm_layer(streams, mods, g, pa,
                                                       (state_mlstm_C, state_mlstm_n, state_mlstm_m), j, n_mlstm,
                                                       c_stack)
            new_n.append(nn)
            new_m.append(mn)
        elif kind == 1:
            pb = (b_w_in[j], b_w_a1[j], b_w_a2[j], b_b_a[j], b_out_g[j], b_w_out[j])
            (yp, ys), sn = _gla_layer(streams, mods, g, pb, state_gla_S[:, j:j + 1])
            new_S.append(sn)
        else:
            pc = (c_w_in[j], c_q_g[j], c_k_g[j], c_rpb[j], c_w_out[j])
            (yp, ys), (kn, vn) = _na_layer(streams, mods, g, pc, cache_na_k[:, j], cache_na_v[:, j])
            new_k.append(kn)
            new_v.append(vn)
    return (yp.reshape(bp, seq, d), ys.reshape(bs, dec_seq, d), c_stack, jnp.stack(new_n, axis=1),
            jnp.stack(new_m, axis=1), jnp.concatenate(new_S, axis=1), jnp.stack(new_k, axis=1),
            jnp.stack(new_v, axis=1))
```

```python
import functools

import jax
import jax.numpy as jnp
from jax import lax
from jax.experimental import pallas as pl
from jax.experimental.pallas import tpu as pltpu

F32 = jnp.float32
BF16 = jnp.bfloat16
EPS = 1e-6

GRID_W = 64
NA_WIN_H = 8
NA_WIN_W = 16
GLA_TAU = 16.0
GLA_RANK = 16
MLSTM_CHUNK = 256
GLA_CHUNK = 64
GLA_DIAG = 8
GLA_HEADS_PER_STEP = 4
LOG2E = 1.4426950408889634
ROW_TILE = 512
V7X_VMEM_LIMIT = 56 * 1024 * 1024


def _params(sem):
    return pltpu.CompilerParams(dimension_semantics=sem, vmem_limit_bytes=V7X_VMEM_LIMIT)


def _dot(a, b):
    return jnp.dot(a, b, preferred_element_type=F32)


def _dot_nt(a, b):
    return lax.dot_general(a, b, (((1,), (1,)), ((), ())), preferred_element_type=F32)


def _dot_tn(a, b):
    return lax.dot_general(a, b, (((0,), (0,)), ((), ())), preferred_element_type=F32)


def _split3(x):
    hi = x.astype(BF16)
    r1 = x - hi.astype(F32)
    mid = r1.astype(BF16)
    lo = (r1 - mid.astype(F32)).astype(BF16)
    return hi, mid, lo


def _dot_exact_lhs(sel, x):
    hi, mid, lo = _split3(x)
    return _dot(sel, hi) + (_dot(sel, mid) + _dot(sel, lo))


def _silu(x):
    return x * jax.nn.sigmoid(x)


def _log_sigmoid(x):
    return jnp.minimum(x, 0.0) - jnp.log(1.0 + jnp.exp(-jnp.abs(x)))


def _mod_kernel(c_ref, w_ref, b_ref, o_ref):
    a = _silu(c_ref[...])
    w = w_ref[0]
    ah = a.astype(BF16)
    al = (a - ah.astype(F32)).astype(BF16)
    wh = w.astype(BF16)
    wl = (w - wh.astype(F32)).astype(BF16)
    o_ref[0] = _dot(ah, wh) + (_dot(al, wh) + _dot(ah, wl)) + b_ref[0]


def _modulation(cs, w_mod, b_mod):
    depth, d, d3 = w_mod.shape
    rows = cs.shape[0]
    return pl.pallas_call(
        _mod_kernel,
        grid=(depth, d3 // d),
        in_specs=[
            pl.BlockSpec((rows, d), lambda l, j: (0, 0)),
            pl.BlockSpec((1, d, d), lambda l, j: (l, 0, j)),
            pl.BlockSpec((1, 1, d), lambda l, j: (l, 0, j)),
        ],
        out_specs=pl.BlockSpec((1, rows, d), lambda l, j: (l, 0, j)),
        out_shape=jax.ShapeDtypeStruct((depth, rows, d3), F32),
        compiler_params=_params(("parallel", "parallel")),
        name="modulation",
    )(cs, w_mod, b_mod.reshape(depth, 1, d3))


def _inproj_kernel(x_ref, mod_ref, g_ref, w_ref, o_ref, *tail_ref, d):
    x = x_ref[...]
    y = x * lax.rsqrt(jnp.mean(x * x, axis=-1, keepdims=True) + EPS) * g_ref[...]
    shift = mod_ref[0, :, 0:d]
    scale = mod_ref[0, :, d:2 * d]
    h = y * (1.0 + scale) + shift
    res = _dot(h.astype(BF16), w_ref[...])
    n_main = o_ref.shape[1]
    o_ref[...] = res[:, 0:n_main].astype(o_ref.dtype)
    if tail_ref:
        tail_ref[0][...] = res[:, n_main:]


def _inproj(x, mod, g, w, seq_len, out_dtype=BF16, n_tail=0):
    m, d = x.shape
    n = w.shape[1]
    n_main = n - n_tail
    tm = min(ROW_TILE, seq_len)
    tiles_per_seq = seq_len // tm
    out_specs = [pl.BlockSpec((tm, n_main), lambda i: (i, 0))]
    out_shape = [jax.ShapeDtypeStruct((m, n_main), out_dtype)]
    if n_tail:
        out_specs.append(pl.BlockSpec((tm, n_tail), lambda i: (i, 0)))
        out_shape.append(jax.ShapeDtypeStruct((m, n_tail), F32))
    if mod.shape[0] == 1:
        mod_map = lambda i: (0, 0, 0)
    else:
        mod_map = lambda i: (i // tiles_per_seq, 0, 0)
    return pl.pallas_call(
        functools.partial(_inproj_kernel, d=d),
        grid=(m // tm,),
        in_specs=[
            pl.BlockSpec((tm, d), lambda i: (i, 0)),
            pl.BlockSpec((1, 1, 3 * d), mod_map),
            pl.BlockSpec((1, d), lambda i: (0, 0)),
            pl.BlockSpec((d, n), lambda i: (0, 0)),
        ],
        out_specs=out_specs,
        out_shape=out_shape,
        compiler_params=_params(("parallel",)),
        name="inproj",
    )(x, mod, g, w)


def _outproj_kernel(u_ref, r_ref, mod_ref, w_ref, y_ref, o_ref, *, d):
    a = u_ref[...].astype(F32) * _silu(r_ref[...].astype(F32))
    out = _dot(a.astype(BF16), w_ref[...])
    gate = mod_ref[0, :, 2 * d:3 * d]
    o_ref[...] = y_ref[...] + gate * out


def _outproj(u, r_arr, r_col_block, mod, w, y, seq_len):
    m, kdim = u.shape
    d = y.shape[1]
    tm = min(ROW_TILE, seq_len)
    tiles_per_seq = seq_len // tm
    if mod.shape[0] == 1:
        mod_map = lambda i: (0, 0, 0)
    else:
        mod_map = lambda i: (i // tiles_per_seq, 0, 0)
    return pl.pallas_call(
        functools.partial(_outproj_kernel, d=d),
        grid=(m // tm,),
        in_specs=[
            pl.BlockSpec((tm, kdim), lambda i: (i, 0)),
            pl.BlockSpec((tm, kdim), lambda i: (i, r_col_block)),
            pl.BlockSpec((1, 1, 3 * d), mod_map),
            pl.BlockSpec((kdim, d), lambda i: (0, 0)),
            pl.BlockSpec((tm, d), lambda i: (i, 0)),
        ],
        out_specs=pl.BlockSpec((tm, d), lambda i: (i, 0)),
        out_shape=jax.ShapeDtypeStruct((m, d), F32),
        compiler_params=_params(("parallel",)),
        name="outproj",
    )(u, r_arr, mod, w, y)


def _mlstm_pre_kernel(xm_ref, prev_ref, next_ref, cw_ref, cb_ref, bdq_ref, bdk_ref, bdv_ref, wg_ref, bg_ref,
                      q_ref, k_ref, v_ref, xc_ref, gc_ref, gr_ref, *, tiles_per_seq, n_heads, q_scale):
    i = pl.program_id(0)
    tm, inner = xm_ref.shape
    xm = xm_ref[...].astype(F32)
    row = lax.broadcasted_iota(jnp.int32, (tm, 1), 0)
    first = (i % tiles_per_seq) == 0
    last = (i % tiles_per_seq) == tiles_per_seq - 1
    halo = prev_ref.shape[0]
    prev_row = jnp.where(first, 0.0, prev_ref[halo - 1:halo, :].astype(F32))
    next_row = jnp.where(last, 0.0, next_ref[0:1, :].astype(F32))
    x_m1 = jnp.where(row == 0, prev_row, pltpu.roll(xm, 1, 0))
    x_p1 = jnp.where(row == tm - 1, next_row, pltpu.roll(xm, tm - 1, 0))
    conv = cw_ref[0:1, :] * x_m1 + cw_ref[1:2, :] * xm + cw_ref[2:3, :] * x_p1 + cb_ref[...]
    xc = _silu(conv)
    xc_ref[...] = xc.astype(xc_ref.dtype)

    xc_b = xc.astype(BF16)
    xm_b = xm.astype(BF16)
    blk = bdq_ref.shape[1]
    acc = jnp.zeros((tm, wg_ref.shape[1]), F32)
    for j in range(inner // blk):
        cols = slice(j * blk, (j + 1) * blk)
        qj = _dot(xc_b[:, cols], bdq_ref[j])
        kj = _dot(xc_b[:, cols], bdk_ref[j])
        vj = _dot(xm_b[:, cols], bdv_ref[j])
        qb, kb, vb = qj.astype(BF16), kj.astype(BF16), vj.astype(BF16)
        q_ref[:, cols] = (qj * q_scale).astype(BF16)
        k_ref[:, cols] = kb
        v_ref[:, cols] = vb
        acc = acc + _dot(qb, wg_ref[cols, :])
        acc = acc + _dot(kb, wg_ref[inner + j * blk:inner + (j + 1) * blk, :])
        acc = acc + _dot(vb, wg_ref[2 * inner + j * blk:2 * inner + (j + 1) * blk, :])
    g = acc + bg_ref[...]
    lane = lax.broadcasted_iota(jnp.int32, g.shape, 1)
    kind = lane % 4
    lf = jnp.where((kind >= 2) & (lane < 4 * n_heads), _log_sigmoid(g), 0.0)
    r = lax.broadcasted_iota(jnp.int32, (tm, tm), 0)
    c = lax.broadcasted_iota(jnp.int32, (tm, tm), 1)
    prefix = _dot_exact_lhs((c <= r).astype(BF16), lf)
    suffix = _dot_exact_lhs((c >= r).astype(BF16), lf)
    gcol = jnp.where(kind < 2, g, jnp.where(kind == 2, prefix, suffix))
    grow = jnp.transpose(gcol)
    for h in range(n_heads):
        gc_ref[h] = gcol[:, 4 * h:4 * h + 4]
        gr_ref[h] = grow[4 * h:4 * h + 4, :]


def _mlstm_pre(xz, conv_w, conv_b, bdq, bdk, bdv, wg, bg, seq_len, n_heads):
    m = xz.shape[0]
    inner = conv_w.shape[1]
    tm = MLSTM_CHUNK
    tiles_per_seq = seq_len // tm
    halo = 16
    last_halo = m // halo - 1
    return pl.pallas_call(
        functools.partial(_mlstm_pre_kernel, tiles_per_seq=tiles_per_seq, n_heads=n_heads,
                          q_scale=(inner // n_heads) ** -0.5),
        grid=(m // tm,),
        in_specs=[
            pl.BlockSpec((tm, inner), lambda i: (i, 0)),
            pl.BlockSpec((halo, inner), lambda i: (jnp.maximum(i * (tm // halo) - 1, 0), 0)),
            pl.BlockSpec((halo, inner), lambda i: (jnp.minimum((i + 1) * (tm // halo), last_halo), 0)),
            pl.BlockSpec(conv_w.shape, lambda i: (0, 0)),
            pl.BlockSpec((1, inner), lambda i: (0, 0)),
            pl.BlockSpec(bdq.shape, lambda i: (0, 0, 0)),
            pl.BlockSpec(bdk.shape, lambda i: (0, 0, 0)),
            pl.BlockSpec(bdv.shape, lambda i: (0, 0, 0)),
            pl.BlockSpec(wg.shape, lambda i: (0, 0)),
            pl.BlockSpec(bg.shape, lambda i: (0, 0)),
        ],
        out_specs=[
            pl.BlockSpec((tm, inner), lambda i: (i, 0)),
            pl.BlockSpec((tm, inner), lambda i: (i, 0)),
            pl.BlockSpec((tm, inner), lambda i: (i, 0)),
            pl.BlockSpec((tm, inner), lambda i: (i, 0)),
            pl.BlockSpec((n_heads, tm, 4), lambda i: (0, i, 0)),
            pl.BlockSpec((n_heads, 4, tm), lambda i: (0, 0, i)),
        ],
        out_shape=[
            jax.ShapeDtypeStruct((m, inner), BF16),
            jax.ShapeDtypeStruct((m, inner), BF16),
            jax.ShapeDtypeStruct((m, inner), BF16),
            jax.ShapeDtypeStruct((m, inner), BF16),
            jax.ShapeDtypeStruct((n_heads, m, 4), F32),
            jax.ShapeDtypeStruct((n_heads, 4, m), F32),
        ],
        compiler_params=_params(("parallel",)),
        name="mlstm_pre",
    )(xz, xz, xz, conv_w, conv_b, bdq, bdk, bdv, wg, bg)


def _mlstm_scan_kernel(*refs, n_chunks, has_state, emit_state, has_alias, state_slot):
    q_ref, k_ref, v_ref, xc_ref, gc_ref, gr_ref, og_ref, skip_ref = refs[:8]
    pos = 8
    if has_state:
        c0_ref, n0_ref, m0_ref = refs[pos:pos + 3]
        pos += 3
    pos += int(has_alias)
    u_ref = refs[pos]
    pos += 1
    if emit_state:
        cn_ref, nn_ref, mn_ref = refs[pos:pos + 3]
        pos += 3
        for other in range(cn_ref.shape[1]):
            if other != state_slot:
                cn_ref[0, other] = jnp.zeros(cn_ref.shape[2:], F32)
    c_scr, n_scr, m_scr, acc_scr = refs[pos:pos + 4]

    L = MLSTM_CHUNK
    dh = q_ref.shape[1]
    row = lax.broadcasted_iota(jnp.int32, (L, L), 0)
    col = lax.broadcasted_iota(jnp.int32, (L, L), 1)

    def gates(rows, d):
        gc = gc_ref[0, rows, :]
        gr = gr_ref[0, :, rows]
        ig_col, b_col = gc[:, d:d + 1], gc[:, 2 + d:3 + d]
        ig_row, b_row = gr[d:d + 1, :], gr[2 + d:3 + d, :]
        mask = (col <= row) if d == 0 else (col >= row)
        dmat = jnp.where(mask, b_col - b_row + ig_row, -jnp.inf)
        b_last = b_col[L - 1:L, :] if d == 0 else b_col[0:1, :]
        return dmat, b_col, b_last - b_col + ig_col, b_last

    def finalize(rows, hs):
        mu = jnp.mean(hs, axis=-1, keepdims=True)
        hc = hs - mu
        hn = hc * lax.rsqrt(jnp.mean(hc * hc, axis=-1, keepdims=True) + EPS) * og_ref[...]
        u_ref[rows, :] = (hn + skip_ref[...] * xc_ref[rows, :].astype(F32)).astype(u_ref.dtype)

    if n_chunks == 1 and not has_state:
        rows = pl.ds(0, L)
        q, k, v = q_ref[...], k_ref[...], v_ref[...]
        s_qk = _dot_nt(q, k)
        p = None
        for d in range(2):
            dmat, b_col, w_state, b_last = gates(rows, d)
            m_t = jnp.max(dmat, axis=1, keepdims=True)
            s = s_qk * jnp.exp(dmat - m_t)
            den = jnp.sum(s, axis=1, keepdims=True)
            s = s / jnp.maximum(jnp.abs(den), jnp.exp(-m_t))
            p = s if p is None else p + s
            if emit_state:
                m_new = jnp.maximum(b_last, jnp.max(w_state, axis=0, keepdims=True))
                kw = k.astype(F32) * jnp.exp(w_state - m_new)
                cn_ref[0, state_slot, d, 0] = _dot_tn(kw.astype(BF16), v)
                nn_ref[0, d, 0] = jnp.sum(kw, axis=0, keepdims=True)
                mn_ref[0, d, 0] = jnp.broadcast_to(m_new, mn_ref.shape[3:])
        finalize(rows, _dot(p.astype(BF16), v))
        return

    assert n_chunks % 2 == 0

    def direction(c, d, finish, update):
        rows = pl.ds(pl.multiple_of(c * L, L), L)
        q = q_ref[rows, :]
        k = k_ref[rows, :]
        v = v_ref[rows, :]
        dmat, b_col, w_state, b_last = gates(rows, d)
        m_prev = m_scr[d]
        c_prev = c_scr[d]
        n_prev = n_scr[d]
        inter = b_col + m_prev
        m_t = jnp.maximum(inter, jnp.max(dmat, axis=1, keepdims=True))
        s = _dot_nt(q, k) * jnp.exp(dmat - m_t)
        w_inter = jnp.exp(inter - m_t)
        num = _dot(s.astype(BF16), v) + w_inter * _dot(q, c_prev.astype(BF16))
        den = (jnp.sum(s, axis=1, keepdims=True)
               + w_inter * jnp.sum(q.astype(F32) * n_prev, axis=1, keepdims=True))
        h = num / jnp.maximum(jnp.abs(den), jnp.exp(-m_t))
        if finish:
            finalize(rows, acc_scr[rows, :] + h)
        else:
            acc_scr[rows, :] = h
        if update:
            m_new = jnp.maximum(b_last + m_prev, jnp.max(w_state, axis=0, keepdims=True))
            decay = jnp.exp(b_last + m_prev - m_new)
            kw = k.astype(F32) * jnp.exp(w_state - m_new)
            c_scr[d] = decay * c_prev + _dot_tn(kw.astype(BF16), v)
            n_scr[d] = decay * n_prev + jnp.sum(kw, axis=0, keepdims=True)
            m_scr[d] = m_new

    def step(s, finish, update):
        direction(s, 0, finish, update)
        direction(n_chunks - 1 - s, 1, finish, update)

    for d in range(2):
        if has_state:
            c_scr[d] = c0_ref[0, 0, d, 0]
            n_scr[d] = n0_ref[0, 0, d, 0]
            m_scr[d] = m0_ref[0, 0, d, 0][:, 0:1]
        else:
            c_scr[d] = jnp.zeros((dh, dh), F32)
            n_scr[d] = jnp.zeros((1, dh), F32)
            m_scr[d] = jnp.zeros((1, 1), F32)

    half = n_chunks // 2

    def first_half(s, carry):
        step(s, False, True)
        return carry

    def second_half(s, carry):
        step(s, True, True)
        return carry

    lax.fori_loop(0, half, first_half, 0)
    lax.fori_loop(half, n_chunks - 1, second_half, 0)
    step(n_chunks - 1, True, emit_state)
    if emit_state:
        for d in range(2):
            cn_ref[0, state_slot, d, 0] = c_scr[d]
            nn_ref[0, d, 0] = n_scr[d]
            mn_ref[0, d, 0] = jnp.broadcast_to(m_scr[d], mn_ref.shape[3:])


def _mlstm_scan(q, k, v, xc, gch, grh, out_g, skip, state, layer_idx, n_layers, c_stack, n_seq, seq_len, n_heads,
                emit_state):
    m, inner = q.shape
    dh = inner // n_heads
    n_chunks = seq_len // MLSTM_CHUNK
    has_state = state is not None
    j = layer_idx
    tok = lambda b, h: (b, h)
    in_specs = [
        pl.BlockSpec((seq_len, dh), tok),
        pl.BlockSpec((seq_len, dh), tok),
        pl.BlockSpec((seq_len, dh), tok),
        pl.BlockSpec((seq_len, dh), tok),
        pl.BlockSpec((1, seq_len, 4), lambda b, h: (h, b, 0)),
        pl.BlockSpec((1, 4, seq_len), lambda b, h: (h, 0, b)),
        pl.BlockSpec((1, dh), lambda b, h: (0, h)),
        pl.BlockSpec((1, dh), lambda b, h: (0, h)),
    ]
    args = [q, k, v, xc, gch, grh, out_g, skip]
    if has_state:
        c0, n0, m0 = state
        in_specs += [
            pl.BlockSpec((1, 1, 2, 1, dh, dh), lambda b, h: (b, j, 0, h, 0, 0)),
            pl.BlockSpec((1, 1, 2, 1, 1, dh), lambda b, h: (b, j, 0, h, 0, 0)),
            pl.BlockSpec((1, 1, 2, 1, 1, 128), lambda b, h: (b, j, 0, h, 0, 0)),
        ]
        args += [c0, n0, m0]
    out_specs = [pl.BlockSpec((seq_len, dh), tok)]
    out_shape = [jax.ShapeDtypeStruct((m, inner), BF16)]
    aliases = {}
    state_slot = 0
    if emit_state:
        if c_stack is None:
            c_spec = pl.BlockSpec((1, n_layers, 2, 1, dh, dh), lambda b, h: (b, 0, 0, h, 0, 0))
            state_slot = j
        else:
            c_spec = pl.BlockSpec((1, 1, 2, 1, dh, dh), lambda b, h: (b, j, 0, h, 0, 0))
        out_specs += [
            c_spec,
            pl.BlockSpec((1, 2, 1, 1, dh), lambda b, h: (b, 0, h, 0, 0)),
            pl.BlockSpec((1, 2, 1, 1, 128), lambda b, h: (b, 0, h, 0, 0)),
        ]
        out_shape += [
            jax.ShapeDtypeStruct((n_seq, n_layers, 2, n_heads, dh, dh), F32),
            jax.ShapeDtypeStruct((n_seq, 2, n_heads, 1, dh), F32),
            jax.ShapeDtypeStruct((n_seq, 2, n_heads, 1, 128), F32),
        ]
        if c_stack is not None:
            in_specs.append(pl.BlockSpec(memory_space=pl.ANY))
            args.append(c_stack)
            aliases = {len(args) - 1: 1}
    return pl.pallas_call(
        functools.partial(_mlstm_scan_kernel, n_chunks=n_chunks, has_state=has_state, emit_state=emit_state,
                          has_alias=bool(aliases), state_slot=state_slot),
        grid=(n_seq, n_heads),
        in_specs=in_specs,
        out_specs=out_specs,
        out_shape=out_shape,
        input_output_aliases=aliases,
        scratch_shapes=[pltpu.VMEM((2, dh, dh), F32), pltpu.VMEM((2, 1, dh), F32), pltpu.VMEM((2, 1, 1), F32),
                        pltpu.VMEM((seq_len if n_chunks > 1 else 8, dh), F32)],
        compiler_params=_params(("parallel", "parallel")),
        name="mlstm_scan",
    )(*args)


def _gla_scan_kernel(*refs, n_chunks, heads, has_state, emit_state):
    q_ref, k_ref, v_ref, a_ref, w2_ref, ba_ref, og_ref, dsel_ref = refs[:8]
    pos = 8
    if has_state:
        s0_ref = refs[pos]
        pos += 1
    o_ref = refs[pos]
    pos += 1
    if emit_state:
        sn_ref = refs[pos]
        pos += 1
    st_scr, acc_scr = refs[pos:pos + 2]

    L = GLA_CHUNK
    DB = GLA_DIAG
    dk = q_ref.shape[1] // heads
    dv = v_ref.shape[1] // heads
    qscale = dk ** -0.5
    row = lax.broadcasted_iota(jnp.int32, (L, L), 0)
    col = lax.broadcasted_iota(jnp.int32, (L, L), 1)
    sub = lax.broadcasted_iota(jnp.int32, (L // DB, DB, 1), 1)
    same_diag = (row // DB) == (col // DB)

    def direction(c, d, finish, update):
        rev = d == 1
        rows = pl.ds(pl.multiple_of(c * L, L), L)
        la = _log_sigmoid(_dot(a_ref[rows, :].astype(BF16), w2_ref[d]) + ba_ref[d]) * (LOG2E / GLA_TAU)
        tri = jnp.where((col >= row) if rev else (col <= row), 1.0, 0.0).astype(BF16)
        b_all = _dot_exact_lhs(tri, la)
        for hh in range(heads):
            kcols = slice(hh * dk, (hh + 1) * dk)
            vcols = slice(hh * dv, (hh + 1) * dv)
            q = q_ref[rows, kcols].astype(F32) * qscale
            k = k_ref[rows, kcols].astype(F32)
            v = v_ref[rows, vcols].astype(BF16)
            b = b_all[:, kcols]
            st = st_scr[d, hh]

            o = _dot_nt((q * jnp.exp2(b)).astype(BF16), st.astype(BF16))

            att = jnp.zeros((L, L), F32)
            blk = L // 2
            while blk >= DB:
                pieces = []
                for g in range(L // (2 * blk)):
                    idx = g * 2 * blk + (blk if rev else blk - 1)
                    pieces.append(jnp.broadcast_to(b[idx:idx + 1, :], (2 * blk, dk)))
                ref_rows = pieces[0] if len(pieces) == 1 else jnp.concatenate(pieces, axis=0)
                e = jnp.exp2(-jnp.abs(b - ref_rows))
                a_lvl = _dot_nt((q * e).astype(BF16), (k * e).astype(BF16))
                rb, cb = row // blk, col // blk
                if rev:
                    pair = ((rb % 2) == 0) & (cb == rb + 1)
                else:
                    pair = ((rb % 2) == 1) & (cb == rb - 1)
                att = jnp.where(pair, a_lvl, att)
                blk //= 2

            q3 = q.reshape(L // DB, DB, dk)
            k3 = k.reshape(L // DB, DB, dk)
            b3 = b.reshape(L // DB, DB, dk)
            parts = []
            for j in range(DB):
                valid = (sub <= j) if rev else (sub >= j)
                x = q3 * k3[:, j:j + 1, :] * jnp.exp2(jnp.where(valid, b3 - b3[:, j:j + 1, :], -jnp.inf))
                parts.append(x.reshape(L, dk).astype(BF16))
            diag = _dot(jnp.concatenate(parts, axis=1), dsel_ref[...])
            att = jnp.where(same_diag, diag, att)

            o = o + _dot(att.astype(BF16), v)
            if not finish:
                acc_scr[rows, vcols] = o
            else:
                os_ = acc_scr[rows, vcols] + o
                o_ref[rows, vcols] = (os_ * lax.rsqrt(jnp.mean(os_ * os_, axis=-1, keepdims=True) + EPS)
                                      * og_ref[:, vcols]).astype(o_ref.dtype)
            if update:
                b_last = b[0:1, :] if rev else b[L - 1:L, :]
                kt = (k * jnp.exp2(b_last - b)).astype(BF16)
                st_scr[d, hh] = st * jnp.exp2(b_last) + _dot_tn(v, kt)

    def step(s, finish, update):
        direction(s, 0, finish, update)
        direction(n_chunks - 1 - s, 1, finish, update)

    for d in range(2):
        for hh in range(heads):
            if has_state:
                st_scr[d, hh] = jnp.transpose(s0_ref[0, 0, d, hh])
            else:
                st_scr[d, hh] = jnp.zeros((dv, dk), F32)

    half = n_chunks // 2

    def first_half(s, carry):
        step(s, False, True)
        return carry

    def second_half(s, carry):
        step(s, True, True)
        return carry

    lax.fori_loop(0, half, first_half, 0)
    lax.fori_loop(half, n_chunks - 1, second_half, 0)
    step(n_chunks - 1, True, emit_state)
    if emit_state:
        for d in range(2):
            for hh in range(heads):
                sn_ref[0, 0, d, hh] = jnp.transpose(st_scr[d, hh])


def _gla_scan(proj, aproj, w2pad, b_a, out_g, dsel, state, n_seq, seq_len, n_heads, dk_total, dv_total, emit_state):
    m = proj.shape[0]
    heads = GLA_HEADS_PER_STEP
    dk = dk_total // n_heads
    dv = dv_total // n_heads
    n_chunks = seq_len // GLA_CHUNK
    assert n_chunks % 2 == 0 and n_heads % heads == 0
    has_state = state is not None
    wk, wv = heads * dk, heads * dv
    in_specs = [
        pl.BlockSpec((seq_len, wk), lambda b, h: (b, h)),
        pl.BlockSpec((seq_len, wk), lambda b, h: (b, dk_total // wk + h)),
        pl.BlockSpec((seq_len, wv), lambda b, h: (b, 2 * dk_total // wv + h)),
        pl.BlockSpec((seq_len, 128), lambda b, h: (b, 0)),
        pl.BlockSpec((2, 128, wk), lambda b, h: (0, 0, h)),
        pl.BlockSpec((2, 1, wk), lambda b, h: (0, 0, h)),
        pl.BlockSpec((1, wv), lambda b, h: (0, h)),
        pl.BlockSpec(dsel.shape, lambda b, h: (0, 0)),
    ]
    args = [proj, proj, proj, aproj, w2pad, b_a, out_g, dsel]
    if has_state:
        in_specs.append(pl.BlockSpec((1, 1, 2, heads, dk, dv), lambda b, h: (b, 0, 0, h, 0, 0)))
        args.append(state)
    out_specs = [pl.BlockSpec((seq_len, wv), lambda b, h: (b, h))]
    out_shape = [jax.ShapeDtypeStruct((m, dv_total), BF16)]
    if emit_state:
        out_specs.append(pl.BlockSpec((1, 1, 2, heads, dk, dv), lambda b, h: (b, 0, 0, h, 0, 0)))
        out_shape.append(jax.ShapeDtypeStruct((n_seq, 1, 2, n_heads, dk, dv), F32))
    return pl.pallas_call(
        functools.partial(_gla_scan_kernel, n_chunks=n_chunks, heads=heads, has_state=has_state,
                          emit_state=emit_state),
        grid=(n_seq, n_heads // heads),
        in_specs=in_specs,
        out_specs=out_specs,
        out_shape=out_shape,
        scratch_shapes=[pltpu.VMEM((2, heads, dv, dk), F32), pltpu.VMEM((seq_len, wv), F32)],
        compiler_params=_params(("parallel", "parallel")),
        name="gla_scan",
    )(*args)


def _pair_rmsnorm(x, g, lane_lo):
    w = x.shape[-1]
    r = lax.broadcasted_iota(jnp.int32, (w, w), 0) < w // 2
    c = lax.broadcasted_iota(jnp.int32, (w, w), 1) < w // 2
    same_half = jnp.where(r == c, 1.0, 0.0).astype(BF16)
    x2 = x * x
    hi = x2.astype(BF16)
    lo = (x2 - hi.astype(F32)).astype(BF16)
    ms = (_dot(hi, same_half) + _dot(lo, same_half)) * (2.0 / w)
    return x * lax.rsqrt(ms + EPS) * g


def _na_ctx_kernel(q_ref, k_ref, v_ref, qg_ref, kg_ref, o_ref, kn_ref, vn_ref):
    t, w = q_ref.shape
    dh = w // 2
    scale = dh ** -0.5
    lane_lo = lax.broadcasted_iota(jnp.int32, (1, w), 1) < dh
    qn = _pair_rmsnorm(q_ref[...].astype(F32), qg_ref[...], lane_lo)
    kn = _pair_rmsnorm(k_ref[...].astype(F32), kg_ref[...], lane_lo)
    v = v_ref[...].astype(F32)
    for hh in range(2):
        kn_ref[0, 0, hh] = kn[:, hh * dh:(hh + 1) * dh]
        vn_ref[0, 0, hh] = v[:, hh * dh:(hh + 1) * dh]
    kb = kn.astype(BF16)
    vb = v.astype(BF16)
    outs = []
    for lo in (True, False):
        sel = lane_lo if lo else jnp.logical_not(lane_lo)
        qh = jnp.where(sel, qn, 0.0).astype(BF16)
        s = _dot_nt(qh, kb) * scale
        p = jnp.exp(s - jnp.max(s, axis=-1, keepdims=True))
        l = jnp.sum(p, axis=-1, keepdims=True)
        outs.append(_dot(p.astype(BF16), vb) / l)
    o_ref[...] = jnp.where(lane_lo, outs[0], outs[1]).astype(o_ref.dtype)


def _na_context(proj, q_g2, k_g2, n_seq, seq_len, d_model):
    m = proj.shape[0]
    nb = d_model // 128
    dh = 128 // 2
    return pl.pallas_call(
        _na_ctx_kernel,
        grid=(n_seq, nb),
        in_specs=[
            pl.BlockSpec((seq_len, 128), lambda b, p: (b, p)),
            pl.BlockSpec((seq_len, 128), lambda b, p: (b, nb + p)),
            pl.BlockSpec((seq_len, 128), lambda b, p: (b, 2 * nb + p)),
            pl.BlockSpec((1, 128), lambda b, p: (0, 0)),
            pl.BlockSpec((1, 128), lambda b, p: (0, 0)),
        ],
        out_specs=[
            pl.BlockSpec((seq_len, 128), lambda b, p: (b, p)),
            pl.BlockSpec((1, 1, 2, seq_len, dh), lambda b, p: (b, 0, p, 0, 0)),
            pl.BlockSpec((1, 1, 2, seq_len, dh), lambda b, p: (b, 0, p, 0, 0)),
        ],
        out_shape=[jax.ShapeDtypeStruct((m, d_model), BF16),
                   jax.ShapeDtypeStruct((n_seq, 1, 2 * nb, seq_len, dh), F32),
                   jax.ShapeDtypeStruct((n_seq, 1, 2 * nb, seq_len, dh), F32)],
        compiler_params=_params(("parallel", "parallel")),
        name="na_context",
    )(proj, proj, proj, q_g2, k_g2)


def _na_latent_kernel(q_ref, k_ref, v_ref, kc_ref, vc_ref, bias_ref, qg_ref, kg_ref, o_ref, qn_scr, kn_scr, vb_scr,
                      kc_scr, vc_scr, s_scr, m_scr, p_scr, l_scr, *, n_rows, win_h):
    t, w = q_ref.shape
    dh = w // 2
    gw = t // n_rows
    lane_lo = lax.broadcasted_iota(jnp.int32, (1, w), 1) < dh
    qn_scr[...] = _pair_rmsnorm(q_ref[...].astype(F32), qg_ref[...], lane_lo) * (dh ** -0.5 * LOG2E)
    kn_scr[...] = _pair_rmsnorm(k_ref[...].astype(F32), kg_ref[...], lane_lo).astype(BF16)
    vb_scr[...] = v_ref[...].astype(BF16)
    kc_scr[...] = jnp.concatenate([kc_ref[0, 0], kc_ref[0, 1]], axis=1).astype(BF16)
    vc_scr[...] = jnp.concatenate([vc_ref[0, 0], vc_ref[0, 1]], axis=1).astype(BF16)
    n_loc = win_h * gw

    def window(r):
        r0 = jnp.clip(r - win_h // 2, 0, n_rows - win_h)
        return r0, pl.ds(pl.multiple_of(r0 * gw, gw), n_loc)

    def scores(r, slot):
        r0, krows = window(r)
        qr = qn_scr[pl.ds(pl.multiple_of(r * gw, gw), gw), :]
        q2 = jnp.concatenate([jnp.where(lane_lo, qr, 0.0), jnp.where(lane_lo, 0.0, qr)], axis=0).astype(BF16)
        s_loc = _dot_nt(q2, kn_scr[krows, :]) + bias_ref[0, r - r0]
        s_ctx = _dot_nt(q2, kc_scr[...])
        s_scr[slot, :, 0:n_loc] = s_loc
        s_scr[slot, :, n_loc:] = s_ctx
        m_scr[slot] = jnp.maximum(jnp.max(s_loc, axis=-1, keepdims=True), jnp.max(s_ctx, axis=-1, keepdims=True))

    def probabilities(slot):
        p = jnp.exp2(s_scr[slot] - m_scr[slot])
        l_scr[slot] = jnp.sum(p, axis=-1, keepdims=True)
        p_scr[slot] = p.astype(BF16)

    def output(r, slot):
        _, krows = window(r)
        o2 = (_dot(p_scr[slot, :, 0:n_loc], vb_scr[krows, :]) + _dot(p_scr[slot, :, n_loc:], vc_scr[...])) / l_scr[slot]
        o_ref[pl.ds(pl.multiple_of(r * gw, gw), gw), :] = jnp.where(lane_lo, o2[0:gw], o2[gw:]).astype(o_ref.dtype)

    scores(0, 0)
    scores(1, 1)
    probabilities(0)

    def body(ii, carry):
        for slot in range(2):
            i = 2 * ii + slot
            output(i, slot)
            probabilities(1 - slot)
            scores(i + 2, slot)
        return carry

    lax.fori_loop(0, n_rows // 2 - 1, body, 0)
    output(n_rows - 2, 0)
    probabilities(1)
    output(n_rows - 1, 1)


def _na_latent(proj, kc, vc, bias, q_g2, k_g2, n_seq, seq_len, d_model):
    m = proj.shape[0]
    nb = d_model // 128
    n_rows = seq_len // GRID_W
    win_h = min(NA_WIN_H, n_rows)
    past, dh = kc.shape[2:]
    n_keys = win_h * GRID_W + past
    assert n_rows % 2 == 0 and n_rows >= 4
    return pl.pallas_call(
        functools.partial(_na_latent_kernel, n_rows=n_rows, win_h=win_h),
        grid=(n_seq, nb),
        in_specs=[
            pl.BlockSpec((seq_len, 128), lambda b, p: (b, p)),
            pl.BlockSpec((seq_len, 128), lambda b, p: (b, nb + p)),
            pl.BlockSpec((seq_len, 128), lambda b, p: (b, 2 * nb + p)),
            pl.BlockSpec((1, 2, past, dh), lambda b, p: (b, p, 0, 0)),
            pl.BlockSpec((1, 2, past, dh), lambda b, p: (b, p, 0, 0)),
            pl.BlockSpec((1,) + bias.shape[1:], lambda b, p: (p, 0, 0, 0)),
            pl.BlockSpec((1, 128), lambda b, p: (0, 0)),
            pl.BlockSpec((1, 128), lambda b, p: (0, 0)),
        ],
        out_specs=pl.BlockSpec((seq_len, 128), lambda b, p: (b, p)),
        out_shape=jax.ShapeDtypeStruct((m, d_model), BF16),
        scratch_shapes=[pltpu.VMEM((seq_len, 128), F32), pltpu.VMEM((seq_len, 128), BF16),
                        pltpu.VMEM((seq_len, 128), BF16), pltpu.VMEM((past, 128), BF16),
                        pltpu.VMEM((past, 128), BF16), pltpu.VMEM((2, 2 * GRID_W, n_keys), F32),
                        pltpu.VMEM((2, 2 * GRID_W, 1), F32), pltpu.VMEM((2, 2 * GRID_W, n_keys), BF16),
                        pltpu.VMEM((2, 2 * GRID_W, 1), F32)],
        compiler_params=_params(("parallel", "parallel")),
        name="na_latent",
    )(proj, proj, proj, kc, vc, bias, q_g2, k_g2)


def _na_bias_table(rpb, n_rows):
    win_h = min(NA_WIN_H, n_rows)
    n_h, n_rr, n_cr = rpb.shape
    cols = jnp.arange(GRID_W)
    col_start = jnp.clip(cols - NA_WIN_W // 2, 0, GRID_W - NA_WIN_W)
    in_win = (cols[None, :] >= col_start[:, None]) & (cols[None, :] < col_start[:, None] + NA_WIN_W)
    period = GRID_W + 1
    u = jnp.concatenate([rpb[:, :, NA_WIN_W - 1:], jnp.zeros((n_h, n_rr, period - n_cr), rpb.dtype),
                         rpb[:, :, :NA_WIN_W - 1]], axis=2)
    toe = jnp.tile(u, (1, 1, GRID_W))[:, :, :GRID_W * GRID_W].reshape(n_h, n_rr, GRID_W, GRID_W)
    toe = toe.transpose(0, 2, 1, 3)
    toe = jnp.where(in_win[None, :, None, :], toe * LOG2E, -jnp.inf)
    entries = [toe[:, :, NA_WIN_H - 1 - off:NA_WIN_H - 1 - off + win_h, :]
               .reshape(n_h // 2, 2 * GRID_W, win_h * GRID_W) for off in range(win_h)]
    return jnp.stack(entries, axis=1).astype(F32)


def _block_diag_tiles(w, tile):
    n_blk, s, _ = w.shape
    per = tile // s
    n_tiles = n_blk // per
    rows = w.reshape(n_tiles, per, s, s).transpose(0, 2, 1, 3).reshape(n_tiles, 1, s, tile)
    full = jnp.broadcast_to(rows, (n_tiles, per, s, tile)).reshape(n_tiles, tile, tile)
    blk = jnp.arange(tile) // s
    return jnp.where((blk[:, None] == blk[None, :])[None], full, 0.0)


def _mlstm_layer(streams, mods, norm_g, p, state, j, n_layers, c_stack):
    (w_in, conv_w, conv_b, wq, wk, wv, w_ig, b_ig, w_fg, b_fg, out_g, skip, w_out) = p
    inner = conv_w.shape[1]
    n_heads = b_ig.shape[1]
    w_in_b = w_in.astype(BF16)
    w_out_b = w_out.astype(BF16)
    bdq = _block_diag_tiles(wq, 256).astype(BF16)
    bdk = _block_diag_tiles(wk, 256).astype(BF16)
    bdv = _block_diag_tiles(wv, 256).astype(BF16)
    n_g = 4 * n_heads
    wg = jnp.stack([w_ig[0], w_ig[1], w_fg[0], w_fg[1]], axis=2).reshape(3 * inner, n_g)
    wg = jnp.pad(wg, ((0, 0), (0, 128 - n_g))).astype(BF16)
    bg = jnp.pad(jnp.stack([b_ig[0], b_ig[1], b_fg[0], b_fg[1]], axis=1).reshape(n_g), (0, 128 - n_g)).reshape(1, 128)
    outs, new_state = [], None
    for (y, n_seq, seq_len, is_prompt), mod in zip(streams, mods):
        xz = _inproj(y, mod, norm_g, w_in_b, seq_len)[0]
        q, k, v, xc, gch, grh = _mlstm_pre(xz, conv_w, conv_b.reshape(1, inner), bdq, bdk, bdv, wg, bg, seq_len,
                                           n_heads)
        st = None
        if not is_prompt:
            c0, n0, m0 = state
            st = (c0, n0.reshape(n0.shape[:4] + (1, n0.shape[4])),
                  jnp.broadcast_to(m0[..., None, None], m0.shape + (1, 128)))
        res = _mlstm_scan(q, k, v, xc, gch, grh, out_g.reshape(1, inner), skip.reshape(1, inner), st, j, n_layers,
                          c_stack, n_seq, seq_len, n_heads, emit_state=is_prompt)
        if is_prompt:
            u, cn, nn, mn = res
            new_state = (cn, nn[:, :, :, 0, :], mn[:, :, :, 0, 0])
        else:
            u = res[0]
        outs.append(_outproj(u, xz, 1, mod, w_out_b, y, seq_len))
    return outs, new_state


def _gla_layer(streams, mods, norm_g, p, state):
    (w_in, w_a1, w_a2, b_a, out_g, w_out) = p
    d = w_in.shape[0]
    n_heads = state.shape[3]
    dk_total = w_a2.shape[2]
    dv_total = out_g.shape[0]
    pad = 128 - 2 * GLA_RANK
    w_ext = jnp.concatenate([w_in, w_a1[0], w_a1[1], jnp.zeros((d, pad), F32)], axis=1).astype(BF16)
    w2pad = jnp.zeros((2, 128, dk_total), F32)
    w2pad = w2pad.at[0, 0:GLA_RANK].set(w_a2[0]).at[1, GLA_RANK:2 * GLA_RANK].set(w_a2[1]).astype(BF16)
    w_out_b = w_out.astype(BF16)
    L, DB = GLA_CHUNK, GLA_DIAG
    dk = dk_total // n_heads
    dsel = (jnp.arange(DB * dk)[:, None] // dk == jnp.arange(L)[None, :] % DB).astype(BF16)
    outs, new_state = [], None
    for (y, n_seq, seq_len, is_prompt), mod in zip(streams, mods):
        proj, aproj = _inproj(y, mod, norm_g, w_ext, seq_len, n_tail=128)
        res = _gla_scan(proj, aproj, w2pad, b_a.reshape(2, 1, dk_total), out_g.reshape(1, dv_total), dsel,
                        None if is_prompt else state, n_seq, seq_len, n_heads, dk_total, dv_total,
                        emit_state=is_prompt)
        if is_prompt:
            new_state = res[1]
        outs.append(_outproj(res[0], proj, (2 * dk_total + dv_total) // dv_total, mod, w_out_b, y, seq_len))
    return outs, new_state


def _na_layer(streams, mods, norm_g, p, cache_k, cache_v):
    (w_in, q_g, k_g, rpb, w_out) = p
    d = w_in.shape[0]
    n_heads = rpb.shape[0]
    dh = d // n_heads
    w_in_b = w_in.astype(BF16)
    w_out_b = w_out.astype(BF16)
    q_g2 = jnp.tile(q_g, 2).reshape(1, 2 * dh)
    k_g2 = jnp.tile(k_g, 2).reshape(1, 2 * dh)
    outs, new_kv = [], None
    for (y, n_seq, seq_len, is_prompt), mod in zip(streams, mods):
        proj = _inproj(y, mod, norm_g, w_in_b, seq_len, out_dtype=F32 if is_prompt else BF16)[0]
        if is_prompt:
            o, kn, vn = _na_context(proj, q_g2, k_g2, n_seq, seq_len, d)
            new_kv = (kn, vn)
        else:
            bias = _na_bias_table(rpb, seq_len // GRID_W)
            o = _na_latent(proj, cache_k, cache_v, bias, q_g2, k_g2, n_seq, seq_len, d)
        outs.append(_outproj(o, proj, 3, mod, w_out_b, y, seq_len))
    return outs, new_kv


def kernel(x_prompt, x_sample, state_mlstm_C, state_mlstm_n, state_mlstm_m, state_gla_S, cache_na_k, cache_na_v, c, c_ctx, norm_g, w_mod, b_mod, a_w_in, a_conv_w, a_conv_b, a_wq, a_wk, a_wv, a_w_ig, a_b_ig, a_w_fg, a_b_fg, a_out_g, a_skip, a_w_out, b_w_in, b_w_a1, b_w_a2, b_b_a, b_out_g, b_w_out, c_w_in, c_q_g, c_k_g, c_rpb, c_w_out):
    bp, seq, d = x_prompt.shape
    bs, dec_seq, _ = x_sample.shape
    depth = w_mod.shape[0]
    n_mlstm = state_mlstm_C.shape[1]

    cs = jnp.zeros((16, d), F32).at[0:bs].set(c).at[bs].set(c_ctx)
    mod_all = _modulation(cs, w_mod, b_mod)

    yp = x_prompt.reshape(bp * seq, d)
    ys = x_sample.reshape(bs * dec_seq, d)
    c_stack = None
    new_n, new_m, new_S, new_k, new_v = [], [], [], [], []
    for l in range(depth):
        kind, j = l % 3, l // 3
        mod_p = mod_all[l, bs:bs + 1].reshape(1, 1, 3 * d)
        mod_s = mod_all[l, 0:bs].reshape(bs, 1, 3 * d)
        streams = [(yp, bp, seq, True), (ys, bs, dec_seq, False)]
        mods = [mod_p, mod_s]
        g = norm_g[l].reshape(1, d)
        if kind == 0:
            pa = (a_w_in[j], a_conv_w[j], a_conv_b[j], a_wq[j], a_wk[j], a_wv[j], a_w_ig[j], a_b_ig[j],
                  a_w_fg[j], a_b_fg[j], a_out_g[j], a_skip[j], a_w_out[j])
            (yp, ys), (c_stack, nn, mn) = _mlstm_layer(streams, mods, g, pa,
                                                       (state_mlstm_C, state_mlstm_n, state_mlstm_m), j, n_mlstm,
                                                       c_stack)
            new_n.append(nn)
            new_m.append(mn)
        elif kind == 1:
            pb = (b_w_in[j], b_w_a1[j], b_w_a2[j], b_b_a[j], b_out_g[j], b_w_out[j])
            (yp, ys), sn = _gla_layer(streams, mods, g, pb, state_gla_S[:, j:j + 1])
            new_S.append(sn)
        else:
            pc = (c_w_in[j], c_q_g[j], c_k_g[j], c_rpb[j], c_w_out[j])
            (yp, ys), (kn, vn) = _na_layer(streams, mods, g, pc, cache_na_k[:, j], cache_na_v[:, j])
            new_k.append(kn)
            new_v.append(vn)
    return (yp.reshape(bp, seq, d), ys.reshape(bs, dec_seq, d), c_stack, jnp.stack(new_n, axis=1),
            jnp.stack(new_m, axis=1), jnp.concatenate(new_S, axis=1), jnp.concatenate(new_k, axis=1),
            jnp.concatenate(new_v, axis=1))
```

```python
import functools

import jax
import jax.numpy as jnp
from jax import lax
from jax.experimental import pallas as pl
from jax.experimental.pallas import tpu as pltpu

F32 = jnp.float32
BF16 = jnp.bfloat16
EPS = 1e-6

GRID_W = 64
NA_WIN_H = 8
NA_WIN_W = 16
GLA_TAU = 16.0
GLA_RANK = 16
MLSTM_CHUNK = 256
GLA_CHUNK = 64
GLA_DIAG = 8
GLA_HEADS_PER_STEP = 4
NA_CTX_PAIRS = 4
LOG2E = 1.4426950408889634
ROW_TILE = 512
V7X_VMEM_LIMIT = 56 * 1024 * 1024


def _params(sem):
    return pltpu.CompilerParams(dimension_semantics=sem, vmem_limit_bytes=V7X_VMEM_LIMIT)


def _dot(a, b):
    return jnp.dot(a, b, preferred_element_type=F32)


def _dot_nt(a, b):
    return lax.dot_general(a, b, (((1,), (1,)), ((), ())), preferred_element_type=F32)


def _dot_tn(a, b):
    return lax.dot_general(a, b, (((0,), (0,)), ((), ())), preferred_element_type=F32)


def _split3(x):
    hi = x.astype(BF16)
    r1 = x - hi.astype(F32)
    mid = r1.astype(BF16)
    lo = (r1 - mid.astype(F32)).astype(BF16)
    return hi, mid, lo


def _dot_exact_lhs(sel, x):
    hi, mid, lo = _split3(x)
    return _dot(sel, hi) + (_dot(sel, mid) + _dot(sel, lo))


def _silu(x):
    return x * jax.nn.sigmoid(x)


def _log_sigmoid(x):
    return jnp.minimum(x, 0.0) - jnp.log(1.0 + jnp.exp(-jnp.abs(x)))


def _mod_kernel(c_ref, w_ref, b_ref, o_ref):
    a = _silu(c_ref[...])
    w = w_ref[0]
    ah = a.astype(BF16)
    al = (a - ah.astype(F32)).astype(BF16)
    wh = w.astype(BF16)
    wl = (w - wh.astype(F32)).astype(BF16)
    o_ref[0] = _dot(ah, wh) + (_dot(al, wh) + _dot(ah, wl)) + b_ref[0]


def _modulation(cs, w_mod, b_mod):
    depth, d, d3 = w_mod.shape
    rows = cs.shape[0]
    return pl.pallas_call(
        _mod_kernel,
        grid=(depth, d3 // d),
        in_specs=[
            pl.BlockSpec((rows, d), lambda l, j: (0, 0)),
            pl.BlockSpec((1, d, d), lambda l, j: (l, 0, j)),
            pl.BlockSpec((1, 1, d), lambda l, j: (l, 0, j)),
        ],
        out_specs=pl.BlockSpec((1, rows, d), lambda l, j: (l, 0, j)),
        out_shape=jax.ShapeDtypeStruct((depth, rows, d3), F32),
        compiler_params=_params(("parallel", "parallel")),
        name="modulation",
    )(cs, w_mod, b_mod.reshape(depth, 1, d3))


def _modulated_projection(x, mod_ref, g_ref, w_ref, o_ref, tail_ref, d):
    y = x * lax.rsqrt(jnp.mean(x * x, axis=-1, keepdims=True) + EPS) * g_ref[...]
    shift = mod_ref[0, :, 0:d]
    scale = mod_ref[0, :, d:2 * d]
    h = y * (1.0 + scale) + shift
    res = _dot(h.astype(BF16), w_ref[...])
    n_main = o_ref.shape[1]
    o_ref[...] = res[:, 0:n_main].astype(o_ref.dtype)
    if tail_ref:
        tail_ref[0][...] = res[:, n_main:]


def _inproj_kernel(x_ref, mod_ref, g_ref, w_ref, o_ref, *tail_ref, d):
    _modulated_projection(x_ref[...], mod_ref, g_ref, w_ref, o_ref, tail_ref, d)


def _proj_out_specs(m, tm, n, out_dtype, n_tail):
    specs = [pl.BlockSpec((tm, n - n_tail), lambda i: (i, 0))]
    shapes = [jax.ShapeDtypeStruct((m, n - n_tail), out_dtype)]
    if n_tail:
        specs.append(pl.BlockSpec((tm, n_tail), lambda i: (i, 0)))
        shapes.append(jax.ShapeDtypeStruct((m, n_tail), F32))
    return specs, shapes


def _mod_map(mod, tiles_per_seq):
    if mod.shape[0] == 1:
        return lambda i: (0, 0, 0)
    return lambda i: (i // tiles_per_seq, 0, 0)


def _inproj(x, mod, g, w, seq_len, out_dtype=BF16, n_tail=0):
    m, d = x.shape
    n = w.shape[1]
    tm = min(ROW_TILE, seq_len)
    out_specs, out_shape = _proj_out_specs(m, tm, n, out_dtype, n_tail)
    return pl.pallas_call(
        functools.partial(_inproj_kernel, d=d),
        grid=(m // tm,),
        in_specs=[
            pl.BlockSpec((tm, d), lambda i: (i, 0)),
            pl.BlockSpec((1, 1, 3 * d), _mod_map(mod, seq_len // tm)),
            pl.BlockSpec((1, d), lambda i: (0, 0)),
            pl.BlockSpec((d, n), lambda i: (0, 0)),
        ],
        out_specs=out_specs,
        out_shape=out_shape,
        compiler_params=_params(("parallel",)),
        name="inproj",
    )(x, mod, g, w)


def _outproj_kernel(*refs, d, with_next):
    u_ref, r_ref, mod_ref, w_ref, y_ref = refs[:5]
    outs = refs[8:] if with_next else refs[5:]
    a = u_ref[...].astype(F32) * _silu(r_ref[...].astype(F32))
    out = _dot(a.astype(BF16), w_ref[...])
    gate = mod_ref[0, :, 2 * d:3 * d]
    y_new = y_ref[...] + gate * out
    outs[0][...] = y_new
    if with_next:
        modn_ref, gn_ref, wn_ref = refs[5:8]
        _modulated_projection(y_new, modn_ref, gn_ref, wn_ref, outs[1], outs[2:], d)


def _outproj(u, r_arr, r_col_block, mod, w, y, seq_len, nxt=None):
    m, kdim = u.shape
    d = y.shape[1]
    tm = min(ROW_TILE if nxt is None else ROW_TILE // 2, seq_len)
    tiles_per_seq = seq_len // tm
    in_specs = [
        pl.BlockSpec((tm, kdim), lambda i: (i, 0)),
        pl.BlockSpec((tm, kdim), lambda i: (i, r_col_block)),
        pl.BlockSpec((1, 1, 3 * d), _mod_map(mod, tiles_per_seq)),
        pl.BlockSpec((kdim, d), lambda i: (0, 0)),
        pl.BlockSpec((tm, d), lambda i: (i, 0)),
    ]
    args = [u, r_arr, mod, w, y]
    out_specs = [pl.BlockSpec((tm, d), lambda i: (i, 0))]
    out_shape = [jax.ShapeDtypeStruct((m, d), F32)]
    if nxt is not None:
        mod_n, g_n, w_n, out_dtype, n_tail = nxt
        in_specs += [
            pl.BlockSpec((1, 1, 3 * d), _mod_map(mod_n, tiles_per_seq)),
            pl.BlockSpec((1, d), lambda i: (0, 0)),
            pl.BlockSpec(w_n.shape, lambda i: (0, 0)),
        ]
        args += [mod_n, g_n, w_n]
        specs_n, shapes_n = _proj_out_specs(m, tm, w_n.shape[1], out_dtype, n_tail)
        out_specs += specs_n
        out_shape += shapes_n
    return pl.pallas_call(
        functools.partial(_outproj_kernel, d=d, with_next=nxt is not None),
        grid=(m // tm,),
        in_specs=in_specs,
        out_specs=out_specs,
        out_shape=out_shape,
        compiler_params=_params(("parallel",)),
        name="outproj",
    )(*args)


def _mlstm_pre_kernel(xm_ref, prev_ref, next_ref, cw_ref, cb_ref, bdq_ref, bdk_ref, bdv_ref, wg_ref, bg_ref,
                      q_ref, k_ref, v_ref, xc_ref, gc_ref, gr_ref, *, tiles_per_seq, n_heads, q_scale):
    i = pl.program_id(0)
    tm, inner = xm_ref.shape
    xm = xm_ref[...].astype(F32)
    row = lax.broadcasted_iota(jnp.int32, (tm, 1), 0)
    first = (i % tiles_per_seq) == 0
    last = (i % tiles_per_seq) == tiles_per_seq - 1
    halo = prev_ref.shape[0]
    prev_row = jnp.where(first, 0.0, prev_ref[halo - 1:halo, :].astype(F32))
    next_row = jnp.where(last, 0.0, next_ref[0:1, :].astype(F32))
    x_m1 = jnp.where(row == 0, prev_row, pltpu.roll(xm, 1, 0))
    x_p1 = jnp.where(row == tm - 1, next_row, pltpu.roll(xm, tm - 1, 0))
    conv = cw_ref[0:1, :] * x_m1 + cw_ref[1:2, :] * xm + cw_ref[2:3, :] * x_p1 + cb_ref[...]
    xc = _silu(conv)
    xc_ref[...] = xc.astype(xc_ref.dtype)

    xc_b = xc.astype(BF16)
    xm_b = xm.astype(BF16)
    blk = bdq_ref.shape[1]
    acc = jnp.zeros((tm, wg_ref.shape[1]), F32)
    for j in range(inner // blk):
        cols = slice(j * blk, (j + 1) * blk)
        qj = _dot(xc_b[:, cols], bdq_ref[j])
        kj = _dot(xc_b[:, cols], bdk_ref[j])
        vj = _dot(xm_b[:, cols], bdv_ref[j])
        qb, kb, vb = qj.astype(BF16), kj.astype(BF16), vj.astype(BF16)
        q_ref[:, cols] = (qj * q_scale).astype(BF16)
        k_ref[:, cols] = kb
        v_ref[:, cols] = vb
        acc = acc + _dot(qb, wg_ref[cols, :])
        acc = acc + _dot(kb, wg_ref[inner + j * blk:inner + (j + 1) * blk, :])
        acc = acc + _dot(vb, wg_ref[2 * inner + j * blk:2 * inner + (j + 1) * blk, :])
    g = acc + bg_ref[...]
    lane = lax.broadcasted_iota(jnp.int32, g.shape, 1)
    kind = lane % 4
    lf = jnp.where((kind >= 2) & (lane < 4 * n_heads), _log_sigmoid(g), 0.0)
    r = lax.broadcasted_iota(jnp.int32, (tm, tm), 0)
    c = lax.broadcasted_iota(jnp.int32, (tm, tm), 1)
    prefix = _dot_exact_lhs((c <= r).astype(BF16), lf)
    suffix = _dot_exact_lhs((c >= r).astype(BF16), lf)
    gcol = jnp.where(kind < 2, g, jnp.where(kind == 2, prefix, suffix))
    grow = jnp.transpose(gcol)
    for h in range(n_heads):
        gc_ref[h] = gcol[:, 4 * h:4 * h + 4]
        gr_ref[h] = grow[4 * h:4 * h + 4, :]


def _mlstm_pre(xz, conv_w, conv_b, bdq, bdk, bdv, wg, bg, seq_len, n_heads):
    m = xz.shape[0]
    inner = conv_w.shape[1]
    tm = MLSTM_CHUNK
    tiles_per_seq = seq_len // tm
    halo = 16
    last_halo = m // halo - 1
    return pl.pallas_call(
        functools.partial(_mlstm_pre_kernel, tiles_per_seq=tiles_per_seq, n_heads=n_heads,
                          q_scale=(inner // n_heads) ** -0.5),
        grid=(m // tm,),
        in_specs=[
            pl.BlockSpec((tm, inner), lambda i: (i, 0)),
            pl.BlockSpec((halo, inner), lambda i: (jnp.maximum(i * (tm // halo) - 1, 0), 0)),
            pl.BlockSpec((halo, inner), lambda i: (jnp.minimum((i + 1) * (tm // halo), last_halo), 0)),
            pl.BlockSpec(conv_w.shape, lambda i: (0, 0)),
            pl.BlockSpec((1, inner), lambda i: (0, 0)),
            pl.BlockSpec(bdq.shape, lambda i: (0, 0, 0)),
            pl.BlockSpec(bdk.shape, lambda i: (0, 0, 0)),
            pl.BlockSpec(bdv.shape, lambda i: (0, 0, 0)),
            pl.BlockSpec(wg.shape, lambda i: (0, 0)),
            pl.BlockSpec(bg.shape, lambda i: (0, 0)),
        ],
        out_specs=[
            pl.BlockSpec((tm, inner), lambda i: (i, 0)),
            pl.BlockSpec((tm, inner), lambda i: (i, 0)),
            pl.BlockSpec((tm, inner), lambda i: (i, 0)),
            pl.BlockSpec((tm, inner), lambda i: (i, 0)),
            pl.BlockSpec((n_heads, tm, 4), lambda i: (0, i, 0)),
            pl.BlockSpec((n_heads, 4, tm), lambda i: (0, 0, i)),
        ],
        out_shape=[
            jax.ShapeDtypeStruct((m, inner), BF16),
            jax.ShapeDtypeStruct((m, inner), BF16),
            jax.ShapeDtypeStruct((m, inner), BF16),
            jax.ShapeDtypeStruct((m, inner), BF16),
            jax.ShapeDtypeStruct((n_heads, m, 4), F32),
            jax.ShapeDtypeStruct((n_heads, 4, m), F32),
        ],
        compiler_params=_params(("parallel",)),
        name="mlstm_pre",
    )(xz, xz, xz, conv_w, conv_b, bdq, bdk, bdv, wg, bg)


def _mlstm_scan_kernel(*refs, n_chunks, has_state, emit_state, has_alias, state_slot):
    q_ref, k_ref, v_ref, xc_ref, gc_ref, gr_ref, og_ref, skip_ref = refs[:8]
    pos = 8
    if has_state:
        c0_ref, n0_ref, m0_ref = refs[pos:pos + 3]
        pos += 3
    pos += int(has_alias)
    u_ref = refs[pos]
    pos += 1
    if emit_state:
        cn_ref, nn_ref, mn_ref = refs[pos:pos + 3]
        pos += 3
        for other in range(cn_ref.shape[1]):
            if other != state_slot:
                cn_ref[0, other] = jnp.zeros(cn_ref.shape[2:], F32)
    c_scr, n_scr, m_scr, acc_scr = refs[pos:pos + 4]

    L = MLSTM_CHUNK
    dh = q_ref.shape[1]
    row = lax.broadcasted_iota(jnp.int32, (L, L), 0)
    col = lax.broadcasted_iota(jnp.int32, (L, L), 1)

    def gates(rows, d):
        gc = gc_ref[0, rows, :]
        gr = gr_ref[0, :, rows]
        ig_col, b_col = gc[:, d:d + 1], gc[:, 2 + d:3 + d]
        ig_row, b_row = gr[d:d + 1, :], gr[2 + d:3 + d, :]
        mask = (col <= row) if d == 0 else (col >= row)
        dmat = jnp.where(mask, b_col - b_row + ig_row, -jnp.inf)
        b_last = b_col[L - 1:L, :] if d == 0 else b_col[0:1, :]
        return dmat, b_col, b_last - b_col + ig_col, b_last

    def finalize(rows, hs):
        mu = jnp.mean(hs, axis=-1, keepdims=True)
        hc = hs - mu
        hn = hc * lax.rsqrt(jnp.mean(hc * hc, axis=-1, keepdims=True) + EPS) * og_ref[...]
        u_ref[rows, :] = (hn + skip_ref[...] * xc_ref[rows, :].astype(F32)).astype(u_ref.dtype)

    if n_chunks == 1 and not has_state:
        rows = pl.ds(0, L)
        q, k, v = q_ref[...], k_ref[...], v_ref[...]
        s_qk = _dot_nt(q, k)
        p = None
        for d in range(2):
            dmat, b_col, w_state, b_last = gates(rows, d)
            m_t = jnp.max(dmat, axis=1, keepdims=True)
            s = s_qk * jnp.exp(dmat - m_t)
            den = jnp.sum(s, axis=1, keepdims=True)
            s = s / jnp.maximum(jnp.abs(den), jnp.exp(-m_t))
            p = s if p is None else p + s
            if emit_state:
                m_new = jnp.maximum(b_last, jnp.max(w_state, axis=0, keepdims=True))
                kw = k.astype(F32) * jnp.exp(w_state - m_new)
                cn_ref[0, state_slot, d, 0] = _dot_tn(kw.astype(BF16), v)
                nn_ref[0, d, 0] = jnp.sum(kw, axis=0, keepdims=True)
                mn_ref[0, d, 0] = jnp.broadcast_to(m_new, mn_ref.shape[3:])
        finalize(rows, _dot(p.astype(BF16), v))
        return

    assert n_chunks % 2 == 0

    def direction(c, d, finish, update):
        rows = pl.ds(pl.multiple_of(c * L, L), L)
        q = q_ref[rows, :]
        k = k_ref[rows, :]
        v = v_ref[rows, :]
        dmat, b_col, w_state, b_last = gates(rows, d)
        m_prev = m_scr[d]
        c_prev = c_scr[d]
        n_prev = n_scr[d]
        inter = b_col + m_prev
        m_t = jnp.maximum(inter, jnp.max(dmat, axis=1, keepdims=True))
        s = _dot_nt(q, k) * jnp.exp(dmat - m_t)
        w_inter = jnp.exp(inter - m_t)
        num = _dot(s.astype(BF16), v) + w_inter * _dot(q, c_prev.astype(BF16))
        den = (jnp.sum(s, axis=1, keepdims=True)
               + w_inter * jnp.sum(q.astype(F32) * n_prev, axis=1, keepdims=True))
        h = num / jnp.maximum(jnp.abs(den), jnp.exp(-m_t))
        if finish:
            finalize(rows, acc_scr[rows, :] + h)
        else:
            acc_scr[rows, :] = h
        if update:
            m_new = jnp.maximum(b_last + m_prev, jnp.max(w_state, axis=0, keepdims=True))
            decay = jnp.exp(b_last + m_prev - m_new)
            kw = k.astype(F32) * jnp.exp(w_state - m_new)
            c_scr[d] = decay * c_prev + _dot_tn(kw.astype(BF16), v)
            n_scr[d] = decay * n_prev + jnp.sum(kw, axis=0, keepdims=True)
            m_scr[d] = m_new

    def step(s, finish, update):
        direction(s, 0, finish, update)
        direction(n_chunks - 1 - s, 1, finish, update)

    for d in range(2):
        if has_state:
            c_scr[d] = c0_ref[0, 0, d, 0]
            n_scr[d] = n0_ref[0, 0, d, 0]
            m_scr[d] = m0_ref[0, 0, d, 0][:, 0:1]
        else:
            c_scr[d] = jnp.zeros((dh, dh), F32)
            n_scr[d] = jnp.zeros((1, dh), F32)
            m_scr[d] = jnp.zeros((1, 1), F32)

    half = n_chunks // 2

    def first_half(s, carry):
        step(s, False, True)
        return carry

    def second_half(s, carry):
        step(s, True, True)
        return carry

    lax.fori_loop(0, half, first_half, 0)
    lax.fori_loop(half, n_chunks - 1, second_half, 0)
    step(n_chunks - 1, True, emit_state)
    if emit_state:
        for d in range(2):
            cn_ref[0, state_slot, d, 0] = c_scr[d]
            nn_ref[0, d, 0] = n_scr[d]
            mn_ref[0, d, 0] = jnp.broadcast_to(m_scr[d], mn_ref.shape[3:])


def _mlstm_scan(q, k, v, xc, gch, grh, out_g, skip, state, layer_idx, n_layers, c_stack, n_seq, seq_len, n_heads,
                emit_state):
    m, inner = q.shape
    dh = inner // n_heads
    n_chunks = seq_len // MLSTM_CHUNK
    has_state = state is not None
    j = layer_idx
    tok = lambda b, h: (b, h)
    in_specs = [
        pl.BlockSpec((seq_len, dh), tok),
        pl.BlockSpec((seq_len, dh), tok),
        pl.BlockSpec((seq_len, dh), tok),
        pl.BlockSpec((seq_len, dh), tok),
        pl.BlockSpec((1, seq_len, 4), lambda b, h: (h, b, 0)),
        pl.BlockSpec((1, 4, seq_len), lambda b, h: (h, 0, b)),
        pl.BlockSpec((1, dh), lambda b, h: (0, h)),
        pl.BlockSpec((1, dh), lambda b, h: (0, h)),
    ]
    args = [q, k, v, xc, gch, grh, out_g, skip]
    if has_state:
        c0, n0, m0 = state
        in_specs += [
            pl.BlockSpec((1, 1, 2, 1, dh, dh), lambda b, h: (b, j, 0, h, 0, 0)),
            pl.BlockSpec((1, 1, 2, 1, 1, dh), lambda b, h: (b, j, 0, h, 0, 0)),
            pl.BlockSpec((1, 1, 2, 1, 1, 128), lambda b, h: (b, j, 0, h, 0, 0)),
        ]
        args += [c0, n0, m0]
    out_specs = [pl.BlockSpec((seq_len, dh), tok)]
    out_shape = [jax.ShapeDtypeStruct((m, inner), BF16)]
    aliases = {}
    state_slot = 0
    if emit_state:
        if c_stack is None:
            c_spec = pl.BlockSpec((1, n_layers, 2, 1, dh, dh), lambda b, h: (b, 0, 0, h, 0, 0))
            state_slot = j
        else:
            c_spec = pl.BlockSpec((1, 1, 2, 1, dh, dh), lambda b, h: (b, j, 0, h, 0, 0))
        out_specs += [
            c_spec,
            pl.BlockSpec((1, 2, 1, 1, dh), lambda b, h: (b, 0, h, 0, 0)),
            pl.BlockSpec((1, 2, 1, 1, 128), lambda b, h: (b, 0, h, 0, 0)),
        ]
        out_shape += [
            jax.ShapeDtypeStruct((n_seq, n_layers, 2, n_heads, dh, dh), F32),
            jax.ShapeDtypeStruct((n_seq, 2, n_heads, 1, dh), F32),
            jax.ShapeDtypeStruct((n_seq, 2, n_heads, 1, 128), F32),
        ]
        if c_stack is not None:
            in_specs.append(pl.BlockSpec(memory_space=pl.ANY))
            args.append(c_stack)
            aliases = {len(args) - 1: 1}
    return pl.pallas_call(
        functools.partial(_mlstm_scan_kernel, n_chunks=n_chunks, has_state=has_state, emit_state=emit_state,
                          has_alias=bool(aliases), state_slot=state_slot),
        grid=(n_seq, n_heads),
        in_specs=in_specs,
        out_specs=out_specs,
        out_shape=out_shape,
        input_output_aliases=aliases,
        scratch_shapes=[pltpu.VMEM((2, dh, dh), F32), pltpu.VMEM((2, 1, dh), F32), pltpu.VMEM((2, 1, 1), F32),
                        pltpu.VMEM((seq_len if n_chunks > 1 else 8, dh), F32)],
        compiler_params=_params(("parallel", "parallel")),
        name="mlstm_scan",
    )(*args)


def _gla_scan_kernel(*refs, n_chunks, heads, has_state, emit_state):
    q_ref, k_ref, v_ref, a_ref, w2_ref, ba_ref, og_ref, dsel_ref = refs[:8]
    pos = 8
    if has_state:
        s0_ref = refs[pos]
        pos += 1
    o_ref = refs[pos]
    pos += 1
    if emit_state:
        sn_ref = refs[pos]
        pos += 1
    st_scr, acc_scr = refs[pos:pos + 2]

    L = GLA_CHUNK
    DB = GLA_DIAG
    dk = q_ref.shape[1] // heads
    dv = v_ref.shape[1] // heads
    qscale = dk ** -0.5
    row = lax.broadcasted_iota(jnp.int32, (L, L), 0)
    col = lax.broadcasted_iota(jnp.int32, (L, L), 1)
    sub = lax.broadcasted_iota(jnp.int32, (L // DB, DB, 1), 1)
    same_diag = (row // DB) == (col // DB)

    def direction(c, d, finish, update):
        rev = d == 1
        rows = pl.ds(pl.multiple_of(c * L, L), L)
        la = _log_sigmoid(_dot(a_ref[rows, :].astype(BF16), w2_ref[d]) + ba_ref[d]) * (LOG2E / GLA_TAU)
        tri = jnp.where((col >= row) if rev else (col <= row), 1.0, 0.0).astype(BF16)
        b_all = _dot_exact_lhs(tri, la)
        for hh in range(heads):
            kcols = slice(hh * dk, (hh + 1) * dk)
            vcols = slice(hh * dv, (hh + 1) * dv)
            q = q_ref[rows, kcols].astype(F32) * qscale
            k = k_ref[rows, kcols].astype(F32)
            v = v_ref[rows, vcols].astype(BF16)
            b = b_all[:, kcols]
            st = st_scr[d, hh]

            o = _dot_nt((q * jnp.exp2(b)).astype(BF16), st.astype(BF16))

            att = jnp.zeros((L, L), F32)
            blk = L // 2
            while blk >= DB:
                pieces = []
                for g in range(L // (2 * blk)):
                    idx = g * 2 * blk + (blk if rev else blk - 1)
                    pieces.append(jnp.broadcast_to(b[idx:idx + 1, :], (2 * blk, dk)))
                ref_rows = pieces[0] if len(pieces) == 1 else jnp.concatenate(pieces, axis=0)
                e = jnp.exp2(-jnp.abs(b - ref_rows))
                a_lvl = _dot_nt((q * e).astype(BF16), (k * e).astype(BF16))
                rb, cb = row // blk, col // blk
                if rev:
                    pair = ((rb % 2) == 0) & (cb == rb + 1)
                else:
                    pair = ((rb % 2) == 1) & (cb == rb - 1)
                att = jnp.where(pair, a_lvl, att)
                blk //= 2

            q3 = q.reshape(L // DB, DB, dk)
            k3 = k.reshape(L // DB, DB, dk)
            b3 = b.reshape(L // DB, DB, dk)
            parts = []
            for j in range(DB):
                valid = (sub <= j) if rev else (sub >= j)
                x = q3 * k3[:, j:j + 1, :] * jnp.exp2(jnp.where(valid, b3 - b3[:, j:j + 1, :], -jnp.inf))
                parts.append(x.reshape(L, dk).astype(BF16))
            diag = _dot(jnp.concatenate(parts, axis=1), dsel_ref[...])
            att = jnp.where(same_diag, diag, att)

            o = o + _dot(att.astype(BF16), v)
            if not finish:
                acc_scr[rows, vcols] = o
            else:
                os_ = acc_scr[rows, vcols] + o
                o_ref[rows, vcols] = (os_ * lax.rsqrt(jnp.mean(os_ * os_, axis=-1, keepdims=True) + EPS)
                                      * og_ref[:, vcols]).astype(o_ref.dtype)
            if update:
                b_last = b[0:1, :] if rev else b[L - 1:L, :]
                kt = (k * jnp.exp2(b_last - b)).astype(BF16)
                st_scr[d, hh] = st * jnp.exp2(b_last) + _dot_tn(v, kt)

    def step(s, finish, update):
        direction(s, 0, finish, update)
        direction(n_chunks - 1 - s, 1, finish, update)

    for d in range(2):
        for hh in range(heads):
            if has_state:
                st_scr[d, hh] = jnp.transpose(s0_ref[0, 0, d, hh])
            else:
                st_scr[d, hh] = jnp.zeros((dv, dk), F32)

    half = n_chunks // 2

    def first_half(s, carry):
        step(s, False, True)
        return carry

    def second_half(s, carry):
        step(s, True, True)
        return carry

    lax.fori_loop(0, half, first_half, 0)
    lax.fori_loop(half, n_chunks - 1, second_half, 0)
    step(n_chunks - 1, True, emit_state)
    if emit_state:
        for d in range(2):
            for hh in range(heads):
                sn_ref[0, 0, d, hh] = jnp.transpose(st_scr[d, hh])


def _gla_scan(proj, aproj, w2pad, b_a, out_g, dsel, state, n_seq, seq_len, n_heads, dk_total, dv_total, emit_state):
    m = proj.shape[0]
    heads = GLA_HEADS_PER_STEP
    dk = dk_total // n_heads
    dv = dv_total // n_heads
    n_chunks = seq_len // GLA_CHUNK
    assert n_chunks % 2 == 0 and n_heads % heads == 0
    has_state = state is not None
    wk, wv = heads * dk, heads * dv
    in_specs = [
        pl.BlockSpec((seq_len, wk), lambda b, h: (b, h)),
        pl.BlockSpec((seq_len, wk), lambda b, h: (b, dk_total // wk + h)),
        pl.BlockSpec((seq_len, wv), lambda b, h: (b, 2 * dk_total // wv + h)),
        pl.BlockSpec((seq_len, 128), lambda b, h: (b, 0)),
        pl.BlockSpec((2, 128, wk), lambda b, h: (0, 0, h)),
        pl.BlockSpec((2, 1, wk), lambda b, h: (0, 0, h)),
        pl.BlockSpec((1, wv), lambda b, h: (0, h)),
        pl.BlockSpec(dsel.shape, lambda b, h: (0, 0)),
    ]
    args = [proj, proj, proj, aproj, w2pad, b_a, out_g, dsel]
    if has_state:
        in_specs.append(pl.BlockSpec((1, 1, 2, heads, dk, dv), lambda b, h: (b, 0, 0, h, 0, 0)))
        args.append(state)
    out_specs = [pl.BlockSpec((seq_len, wv), lambda b, h: (b, h))]
    out_shape = [jax.ShapeDtypeStruct((m, dv_total), BF16)]
    if emit_state:
        out_specs.append(pl.BlockSpec((1, 1, 2, heads, dk, dv), lambda b, h: (b, 0, 0, h, 0, 0)))
        out_shape.append(jax.ShapeDtypeStruct((n_seq, 1, 2, n_heads, dk, dv), F32))
    return pl.pallas_call(
        functools.partial(_gla_scan_kernel, n_chunks=n_chunks, heads=heads, has_state=has_state,
                          emit_state=emit_state),
        grid=(n_seq, n_heads // heads),
        in_specs=in_specs,
        out_specs=out_specs,
        out_shape=out_shape,
        scratch_shapes=[pltpu.VMEM((2, heads, dv, dk), F32), pltpu.VMEM((seq_len, wv), F32)],
        compiler_params=_params(("parallel", "parallel")),
        name="gla_scan",
    )(*args)


def _pair_rmsnorm(x, g, lane_lo):
    w = x.shape[-1]
    r = lax.broadcasted_iota(jnp.int32, (w, w), 0) < w // 2
    c = lax.broadcasted_iota(jnp.int32, (w, w), 1) < w // 2
    same_half = jnp.where(r == c, 1.0, 0.0).astype(BF16)
    x2 = x * x
    hi = x2.astype(BF16)
    lo = (x2 - hi.astype(F32)).astype(BF16)
    ms = (_dot(hi, same_half) + _dot(lo, same_half)) * (2.0 / w)
    return x * lax.rsqrt(ms + EPS) * g


def _na_ctx_kernel(q_ref, k_ref, v_ref, qg_ref, kg_ref, o_ref, kn_ref, vn_ref):
    w = qg_ref.shape[1]
    dh = w // 2
    scale = dh ** -0.5
    lane_lo = lax.broadcasted_iota(jnp.int32, (1, w), 1) < dh
    for pair in range(q_ref.shape[1] // w):
        cols = slice(pair * w, (pair + 1) * w)
        qn = _pair_rmsnorm(q_ref[:, cols].astype(F32), qg_ref[...], lane_lo)
        kn = _pair_rmsnorm(k_ref[:, cols].astype(F32), kg_ref[...], lane_lo)
        v = v_ref[:, cols].astype(F32)
        for hh in range(2):
            kn_ref[0, 0, 2 * pair + hh] = kn[:, hh * dh:(hh + 1) * dh]
            vn_ref[0, 0, 2 * pair + hh] = v[:, hh * dh:(hh + 1) * dh]
        kb = kn.astype(BF16)
        vb = v.astype(BF16)
        outs = []
        for lo in (True, False):
            sel = lane_lo if lo else jnp.logical_not(lane_lo)
            qh = jnp.where(sel, qn, 0.0).astype(BF16)
            s = _dot_nt(qh, kb) * scale
            p = jnp.exp(s - jnp.max(s, axis=-1, keepdims=True))
            l = jnp.sum(p, axis=-1, keepdims=True)
            outs.append(_dot(p.astype(BF16), vb) / l)
        o_ref[:, cols] = jnp.where(lane_lo, outs[0], outs[1]).astype(o_ref.dtype)


def _na_context(proj, q_g2, k_g2, n_seq, seq_len, d_model):
    m = proj.shape[0]
    dh = q_g2.shape[1] // 2
    wb = NA_CTX_PAIRS * 2 * dh
    nb = d_model // wb
    return pl.pallas_call(
        _na_ctx_kernel,
        grid=(n_seq, nb),
        in_specs=[
            pl.BlockSpec((seq_len, wb), lambda b, p: (b, p)),
            pl.BlockSpec((seq_len, wb), lambda b, p: (b, nb + p)),
            pl.BlockSpec((seq_len, wb), lambda b, p: (b, 2 * nb + p)),
            pl.BlockSpec((1, 2 * dh), lambda b, p: (0, 0)),
            pl.BlockSpec((1, 2 * dh), lambda b, p: (0, 0)),
        ],
        out_specs=[
            pl.BlockSpec((seq_len, wb), lambda b, p: (b, p)),
            pl.BlockSpec((1, 1, 2 * NA_CTX_PAIRS, seq_len, dh), lambda b, p: (b, 0, p, 0, 0)),
            pl.BlockSpec((1, 1, 2 * NA_CTX_PAIRS, seq_len, dh), lambda b, p: (b, 0, p, 0, 0)),
        ],
        out_shape=[jax.ShapeDtypeStruct((m, d_model), BF16),
                   jax.ShapeDtypeStruct((n_seq, 1, d_model // dh, seq_len, dh), F32),
                   jax.ShapeDtypeStruct((n_seq, 1, d_model // dh, seq_len, dh), F32)],
        compiler_params=_params(("parallel", "parallel")),
        name="na_context",
    )(proj, proj, proj, q_g2, k_g2)


def _na_latent_kernel(q_ref, k_ref, v_ref, kc_ref, vc_ref, bias_ref, qg_ref, kg_ref, o_ref, qn_scr, kn_scr, vb_scr,
                      kc_scr, vc_scr, s_scr, m_scr, p_scr, l_scr, *, n_rows, win_h):
    t, w = q_ref.shape
    dh = w // 2
    gw = t // n_rows
    lane_lo = lax.broadcasted_iota(jnp.int32, (1, w), 1) < dh
    qn_scr[...] = _pair_rmsnorm(q_ref[...].astype(F32), qg_ref[...], lane_lo) * (dh ** -0.5 * LOG2E)
    kn_scr[...] = _pair_rmsnorm(k_ref[...].astype(F32), kg_ref[...], lane_lo).astype(BF16)
    vb_scr[...] = v_ref[...].astype(BF16)
    kc_scr[...] = jnp.concatenate([kc_ref[0, 0], kc_ref[0, 1]], axis=1).astype(BF16)
    vc_scr[...] = jnp.concatenate([vc_ref[0, 0], vc_ref[0, 1]], axis=1).astype(BF16)
    n_loc = win_h * gw

    def window(r):
        r0 = jnp.clip(r - win_h // 2, 0, n_rows - win_h)
        return r0, pl.ds(pl.multiple_of(r0 * gw, gw), n_loc)

    def scores(r, slot):
        r0, krows = window(r)
        qr = qn_scr[pl.ds(pl.multiple_of(r * gw, gw), gw), :]
        q2 = jnp.concatenate([jnp.where(lane_lo, qr, 0.0), jnp.where(lane_lo, 0.0, qr)], axis=0).astype(BF16)
        s_loc = _dot_nt(q2, kn_scr[krows, :]) + bias_ref[0, r - r0]
        s_ctx = _dot_nt(q2, kc_scr[...])
        s_scr[slot, :, 0:n_loc] = s_loc
        s_scr[slot, :, n_loc:] = s_ctx
        m_scr[slot] = jnp.maximum(jnp.max(s_loc, axis=-1, keepdims=True), jnp.max(s_ctx, axis=-1, keepdims=True))

    def probabilities(slot):
        p = jnp.exp2(s_scr[slot] - m_scr[slot])
        l_scr[slot] = jnp.sum(p, axis=-1, keepdims=True)
        p_scr[slot] = p.astype(BF16)

    def output(r, slot):
        _, krows = window(r)
        o2 = (_dot(p_scr[slot, :, 0:n_loc], vb_scr[krows, :]) + _dot(p_scr[slot, :, n_loc:], vc_scr[...])) / l_scr[slot]
        o_ref[pl.ds(pl.multiple_of(r * gw, gw), gw), :] = jnp.where(lane_lo, o2[0:gw], o2[gw:]).astype(o_ref.dtype)

    scores(0, 0)
    scores(1, 1)
    probabilities(0)

    def body(ii, carry):
        for slot in range(2):
            i = 2 * ii + slot
            output(i, slot)
            probabilities(1 - slot)
            scores(i + 2, slot)
        return carry

    lax.fori_loop(0, n_rows // 2 - 1, body, 0)
    output(n_rows - 2, 0)
    probabilities(1)
    output(n_rows - 1, 1)


def _na_latent(proj, kc, vc, bias, q_g2, k_g2, n_seq, seq_len, d_model):
    m = proj.shape[0]
    nb = d_model // 128
    n_rows = seq_len // GRID_W
    win_h = min(NA_WIN_H, n_rows)
    past, dh = kc.shape[2:]
    n_keys = win_h * GRID_W + past
    assert n_rows % 2 == 0 and n_rows >= 4
    return pl.pallas_call(
        functools.partial(_na_latent_kernel, n_rows=n_rows, win_h=win_h),
        grid=(n_seq, nb),
        in_specs=[
            pl.BlockSpec((seq_len, 128), lambda b, p: (b, p)),
            pl.BlockSpec((seq_len, 128), lambda b, p: (b, nb + p)),
            pl.BlockSpec((seq_len, 128), lambda b, p: (b, 2 * nb + p)),
            pl.BlockSpec((1, 2, past, dh), lambda b, p: (b, p, 0, 0)),
            pl.BlockSpec((1, 2, past, dh), lambda b, p: (b, p, 0, 0)),
            pl.BlockSpec((1,) + bias.shape[1:], lambda b, p: (p, 0, 0, 0)),
            pl.BlockSpec((1, 128), lambda b, p: (0, 0)),
            pl.BlockSpec((1, 128), lambda b, p: (0, 0)),
        ],
        out_specs=pl.BlockSpec((seq_len, 128), lambda b, p: (b, p)),
        out_shape=jax.ShapeDtypeStruct((m, d_model), BF16),
        scratch_shapes=[pltpu.VMEM((seq_len, 128), F32), pltpu.VMEM((seq_len, 128), BF16),
                        pltpu.VMEM((seq_len, 128), BF16), pltpu.VMEM((past, 128), BF16),
                        pltpu.VMEM((past, 128), BF16), pltpu.VMEM((2, 2 * GRID_W, n_keys), F32),
                        pltpu.VMEM((2, 2 * GRID_W, 1), F32), pltpu.VMEM((2, 2 * GRID_W, n_keys), BF16),
                        pltpu.VMEM((2, 2 * GRID_W, 1), F32)],
        compiler_params=_params(("parallel", "parallel")),
        name="na_latent",
    )(proj, proj, proj, kc, vc, bias, q_g2, k_g2)


def _na_bias_table(rpb, n_rows):
    win_h = min(NA_WIN_H, n_rows)
    n_h, n_rr, n_cr = rpb.shape
    cols = jnp.arange(GRID_W)
    col_start = jnp.clip(cols - NA_WIN_W // 2, 0, GRID_W - NA_WIN_W)
    in_win = (cols[None, :] >= col_start[:, None]) & (cols[None, :] < col_start[:, None] + NA_WIN_W)
    period = GRID_W + 1
    u = jnp.concatenate([rpb[:, :, NA_WIN_W - 1:], jnp.zeros((n_h, n_rr, period - n_cr), rpb.dtype),
                         rpb[:, :, :NA_WIN_W - 1]], axis=2)
    toe = jnp.tile(u, (1, 1, GRID_W))[:, :, :GRID_W * GRID_W].reshape(n_h, n_rr, GRID_W, GRID_W)
    toe = toe.transpose(0, 2, 1, 3)
    toe = jnp.where(in_win[None, :, None, :], toe * LOG2E, -jnp.inf)
    entries = [toe[:, :, NA_WIN_H - 1 - off:NA_WIN_H - 1 - off + win_h, :]
               .reshape(n_h // 2, 2 * GRID_W, win_h * GRID_W) for off in range(win_h)]
    return jnp.stack(entries, axis=1).astype(F32)


def _block_diag_tiles(w, tile):
    n_blk, s, _ = w.shape
    per = tile // s
    n_tiles = n_blk // per
    rows = w.reshape(n_tiles, per, s, s).transpose(0, 2, 1, 3).reshape(n_tiles, 1, s, tile)
    full = jnp.broadcast_to(rows, (n_tiles, per, s, tile)).reshape(n_tiles, tile, tile)
    blk = jnp.arange(tile) // s
    return jnp.where((blk[:, None] == blk[None, :])[None], full, 0.0)


def _mlstm_layer(streams, projs, p, state, j, n_layers, c_stack):
    (conv_w, conv_b, wq, wk, wv, w_ig, b_ig, w_fg, b_fg, out_g, skip, w_out) = p
    inner = conv_w.shape[1]
    n_heads = b_ig.shape[1]
    w_out_b = w_out.astype(BF16)
    bdq = _block_diag_tiles(wq, 256).astype(BF16)
    bdk = _block_diag_tiles(wk, 256).astype(BF16)
    bdv = _block_diag_tiles(wv, 256).astype(BF16)
    n_g = 4 * n_heads
    wg = jnp.stack([w_ig[0], w_ig[1], w_fg[0], w_fg[1]], axis=2).reshape(3 * inner, n_g)
    wg = jnp.pad(wg, ((0, 0), (0, 128 - n_g))).astype(BF16)
    bg = jnp.pad(jnp.stack([b_ig[0], b_ig[1], b_fg[0], b_fg[1]], axis=1).reshape(n_g), (0, 128 - n_g)).reshape(1, 128)
    outs, new_state = [], None
    for (n_seq, seq_len, is_prompt), (xz,) in zip(streams, projs):
        q, k, v, xc, gch, grh = _mlstm_pre(xz, conv_w, conv_b.reshape(1, inner), bdq, bdk, bdv, wg, bg, seq_len,
                                           n_heads)
        st = None
        if not is_prompt:
            c0, n0, m0 = state
            st = (c0, n0.reshape(n0.shape[:4] + (1, n0.shape[4])),
                  jnp.broadcast_to(m0[..., None, None], m0.shape + (1, 128)))
        res = _mlstm_scan(q, k, v, xc, gch, grh, out_g.reshape(1, inner), skip.reshape(1, inner), st, j, n_layers,
                          c_stack, n_seq, seq_len, n_heads, emit_state=is_prompt)
        if is_prompt:
            u, cn, nn, mn = res
            new_state = (cn, nn[:, :, :, 0, :], mn[:, :, :, 0, 0])
        else:
            u = res[0]
        outs.append((u, xz, 1))
    return outs, new_state, w_out_b


def _gla_in_weights(w_in, w_a1):
    pad = 128 - 2 * GLA_RANK
    return jnp.concatenate([w_in, w_a1[0], w_a1[1], jnp.zeros((w_in.shape[0], pad), F32)], axis=1).astype(BF16)


def _gla_layer(streams, projs, p, state):
    (w_a2, b_a, out_g, w_out) = p
    n_heads = state.shape[3]
    dk_total = w_a2.shape[2]
    dv_total = out_g.shape[0]
    w2pad = jnp.zeros((2, 128, dk_total), F32)
    w2pad = w2pad.at[0, 0:GLA_RANK].set(w_a2[0]).at[1, GLA_RANK:2 * GLA_RANK].set(w_a2[1]).astype(BF16)
    w_out_b = w_out.astype(BF16)
    L, DB = GLA_CHUNK, GLA_DIAG
    dk = dk_total // n_heads
    dsel = (jnp.arange(DB * dk)[:, None] // dk == jnp.arange(L)[None, :] % DB).astype(BF16)
    outs, new_state = [], None
    for (n_seq, seq_len, is_prompt), (proj, aproj) in zip(streams, projs):
        res = _gla_scan(proj, aproj, w2pad, b_a.reshape(2, 1, dk_total), out_g.reshape(1, dv_total), dsel,
                        None if is_prompt else state, n_seq, seq_len, n_heads, dk_total, dv_total,
                        emit_state=is_prompt)
        if is_prompt:
            new_state = res[1]
        outs.append((res[0], proj, (2 * dk_total + dv_total) // dv_total))
    return outs, new_state, w_out_b


def _na_layer(streams, projs, p, cache_k, cache_v):
    (q_g, k_g, rpb, w_out) = p
    d = w_out.shape[0]
    n_heads = rpb.shape[0]
    dh = d // n_heads
    w_out_b = w_out.astype(BF16)
    q_g2 = jnp.tile(q_g, 2).reshape(1, 2 * dh)
    k_g2 = jnp.tile(k_g, 2).reshape(1, 2 * dh)
    outs, new_kv = [], None
    for (n_seq, seq_len, is_prompt), (proj,) in zip(streams, projs):
        if is_prompt:
            o, kn, vn = _na_context(proj, q_g2, k_g2, n_seq, seq_len, d)
            new_kv = (kn, vn)
        else:
            bias = _na_bias_table(rpb, seq_len // GRID_W)
            o = _na_latent(proj, cache_k, cache_v, bias, q_g2, k_g2, n_seq, seq_len, d)
        outs.append((o, proj, 3))
    return outs, new_kv, w_out_b


def kernel(x_prompt, x_sample, state_mlstm_C, state_mlstm_n, state_mlstm_m, state_gla_S, cache_na_k, cache_na_v, c, c_ctx, norm_g, w_mod, b_mod, a_w_in, a_conv_w, a_conv_b, a_wq, a_wk, a_wv, a_w_ig, a_b_ig, a_w_fg, a_b_fg, a_out_g, a_skip, a_w_out, b_w_in, b_w_a1, b_w_a2, b_b_a, b_out_g, b_w_out, c_w_in, c_q_g, c_k_g, c_rpb, c_w_out):
    bp, seq, d = x_prompt.shape
    bs, dec_seq, _ = x_sample.shape
    depth = w_mod.shape[0]
    n_mlstm = state_mlstm_C.shape[1]

    cs = jnp.zeros((16, d), F32).at[0:bs].set(c).at[bs].set(c_ctx)
    mod_all = _modulation(cs, w_mod, b_mod)

    streams = [(bp, seq, True), (bs, dec_seq, False)]
    ys_ = [x_prompt.reshape(bp * seq, d), x_sample.reshape(bs * dec_seq, d)]

    def mods(l):
        return [mod_all[l, bs:bs + 1].reshape(1, 1, 3 * d), mod_all[l, 0:bs].reshape(bs, 1, 3 * d)]

    in_weights = {}

    def in_projection(l, is_prompt):
        kind, j = l % 3, l // 3
        if l not in in_weights:
            in_weights[l] = (_gla_in_weights(b_w_in[j], b_w_a1[j]) if kind == 1
                             else (a_w_in, c_w_in)[kind // 2][j].astype(BF16))
        out_dtype = F32 if (kind == 2 and is_prompt) else BF16
        return in_weights[l], out_dtype, 128 if kind == 1 else 0

    projs = []
    for i, (n_seq, seq_len, is_prompt) in enumerate(streams):
        w0, dt0, tail0 = in_projection(0, is_prompt)
        projs.append(_inproj(ys_[i], mods(0)[i], norm_g[0].reshape(1, d), w0, seq_len, dt0, tail0))

    c_stack = None
    new_n, new_m, new_S, new_k, new_v = [], [], [], [], []
    for l in range(depth):
        kind, j = l % 3, l // 3
        if kind == 0:
            pa = (a_conv_w[j], a_conv_b[j], a_wq[j], a_wk[j], a_wv[j], a_w_ig[j], a_b_ig[j],
                  a_w_fg[j], a_b_fg[j], a_out_g[j], a_skip[j], a_w_out[j])
            outs, (c_stack, nn, mn), w_out_b = _mlstm_layer(
                streams, projs, pa, (state_mlstm_C, state_mlstm_n, state_mlstm_m), j, n_mlstm, c_stack)
            new_n.append(nn)
            new_m.append(mn)
        elif kind == 1:
            pb = (b_w_a2[j], b_b_a[j], b_out_g[j], b_w_out[j])
            outs, sn, w_out_b = _gla_layer(streams, projs, pb, state_gla_S[:, j:j + 1])
            new_S.append(sn)
        else:
            pc = (c_q_g[j], c_k_g[j], c_rpb[j], c_w_out[j])
            outs, (kn, vn), w_out_b = _na_layer(streams, projs, pc, cache_na_k[:, j], cache_na_v[:, j])
            new_k.append(kn)
            new_v.append(vn)
        projs = []
        for i, ((n_seq, seq_len, is_prompt), (u, r_arr, r_blk)) in enumerate(zip(streams, outs)):
            nxt = None
            if l + 1 < depth:
                w_n, dt_n, tail_n = in_projection(l + 1, is_prompt)
                nxt = (mods(l + 1)[i], norm_g[l + 1].reshape(1, d), w_n, dt_n, tail_n)
            res = _outproj(u, r_arr, r_blk, mods(l)[i], w_out_b, ys_[i], seq_len, nxt)
            ys_[i] = res[0]
            projs.append(tuple(res[1:]))
    yp, ys = ys_
    return (yp.reshape(bp, seq, d), ys.reshape(bs, dec_seq, d), c_stack, jnp.stack(new_n, axis=1),
            jnp.stack(new_m, axis=1), jnp.concatenate(new_S, axis=1), jnp.concatenate(new_k, axis=1),
            jnp.concatenate(new_v, axis=1))
```

```python
import functools

import jax
import jax.numpy as jnp
from jax import lax
from jax.experimental import pallas as pl
from jax.experimental.pallas import tpu as pltpu

F32 = jnp.float32
BF16 = jnp.bfloat16
EPS = 1e-6

GRID_W = 64
NA_WIN_H = 8
NA_WIN_W = 16
GLA_TAU = 16.0
GLA_RANK = 16
MLSTM_CHUNK = 256
GLA_CHUNK = 64
GLA_DIAG = 8
GLA_HEADS_PER_STEP = 4
NA_CTX_PAIRS = 4
LOG2E = 1.4426950408889634
ROW_TILE = 512
V7X_VMEM_LIMIT = 56 * 1024 * 1024


def _params(sem):
    return pltpu.CompilerParams(dimension_semantics=sem, vmem_limit_bytes=V7X_VMEM_LIMIT)


def _dot(a, b):
    return jnp.dot(a, b, preferred_element_type=F32)


def _dot_nt(a, b):
    return lax.dot_general(a, b, (((1,), (1,)), ((), ())), preferred_element_type=F32)


def _dot_tn(a, b):
    return lax.dot_general(a, b, (((0,), (0,)), ((), ())), preferred_element_type=F32)


def _split3(x):
    hi = x.astype(BF16)
    r1 = x - hi.astype(F32)
    mid = r1.astype(BF16)
    lo = (r1 - mid.astype(F32)).astype(BF16)
    return hi, mid, lo


def _dot_exact_lhs(sel, x):
    hi, mid, lo = _split3(x)
    return _dot(sel, hi) + (_dot(sel, mid) + _dot(sel, lo))


def _silu(x):
    return x * jax.nn.sigmoid(x)


def _log_sigmoid(x):
    return jnp.minimum(x, 0.0) - jnp.log(1.0 + jnp.exp(-jnp.abs(x)))


def _mod_kernel(c_ref, w_ref, b_ref, o_ref):
    a = _silu(c_ref[...])
    w = w_ref[0]
    ah = a.astype(BF16)
    al = (a - ah.astype(F32)).astype(BF16)
    wh = w.astype(BF16)
    wl = (w - wh.astype(F32)).astype(BF16)
    o_ref[0] = _dot(ah, wh) + (_dot(al, wh) + _dot(ah, wl)) + b_ref[0]


def _modulation(cs, w_mod, b_mod):
    depth, d, d3 = w_mod.shape
    rows = cs.shape[0]
    return pl.pallas_call(
        _mod_kernel,
        grid=(depth, d3 // d),
        in_specs=[
            pl.BlockSpec((rows, d), lambda l, j: (0, 0)),
            pl.BlockSpec((1, d, d), lambda l, j: (l, 0, j)),
            pl.BlockSpec((1, 1, d), lambda l, j: (l, 0, j)),
        ],
        out_specs=pl.BlockSpec((1, rows, d), lambda l, j: (l, 0, j)),
        out_shape=jax.ShapeDtypeStruct((depth, rows, d3), F32),
        compiler_params=_params(("parallel", "parallel")),
        name="modulation",
    )(cs, w_mod, b_mod.reshape(depth, 1, d3))


def _modulated_projection(x, mod_ref, g_ref, w_ref, o_ref, tail_ref, d):
    y = x * lax.rsqrt(jnp.mean(x * x, axis=-1, keepdims=True) + EPS) * g_ref[...]
    shift = mod_ref[0, :, 0:d]
    scale = mod_ref[0, :, d:2 * d]
    h = y * (1.0 + scale) + shift
    res = _dot(h.astype(BF16), w_ref[...])
    n_main = o_ref.shape[1]
    o_ref[...] = res[:, 0:n_main].astype(o_ref.dtype)
    if tail_ref:
        tail_ref[0][...] = res[:, n_main:]


def _inproj_kernel(x_ref, mod_ref, g_ref, w_ref, o_ref, *tail_ref, d):
    _modulated_projection(x_ref[...], mod_ref, g_ref, w_ref, o_ref, tail_ref, d)


def _proj_out_specs(m, tm, n, out_dtype, n_tail):
    specs = [pl.BlockSpec((tm, n - n_tail), lambda i: (i, 0))]
    shapes = [jax.ShapeDtypeStruct((m, n - n_tail), out_dtype)]
    if n_tail:
        specs.append(pl.BlockSpec((tm, n_tail), lambda i: (i, 0)))
        shapes.append(jax.ShapeDtypeStruct((m, n_tail), F32))
    return specs, shapes


def _mod_map(mod, tiles_per_seq):
    if mod.shape[0] == 1:
        return lambda i: (0, 0, 0)
    return lambda i: (i // tiles_per_seq, 0, 0)


def _inproj(x, mod, g, w, seq_len, out_dtype=BF16, n_tail=0):
    m, d = x.shape
    n = w.shape[1]
    tm = min(ROW_TILE, seq_len)
    out_specs, out_shape = _proj_out_specs(m, tm, n, out_dtype, n_tail)
    return pl.pallas_call(
        functools.partial(_inproj_kernel, d=d),
        grid=(m // tm,),
        in_specs=[
            pl.BlockSpec((tm, d), lambda i: (i, 0)),
            pl.BlockSpec((1, 1, 3 * d), _mod_map(mod, seq_len // tm)),
            pl.BlockSpec((1, d), lambda i: (0, 0)),
            pl.BlockSpec((d, n), lambda i: (0, 0)),
        ],
        out_specs=out_specs,
        out_shape=out_shape,
        compiler_params=_params(("parallel",)),
        name="inproj",
    )(x, mod, g, w)


def _outproj_kernel(*refs, d, with_next):
    u_ref, r_ref, mod_ref, w_ref, y_ref = refs[:5]
    outs = refs[8:] if with_next else refs[5:]
    a = u_ref[...].astype(F32) * _silu(r_ref[...].astype(F32))
    out = _dot(a.astype(BF16), w_ref[...])
    gate = mod_ref[0, :, 2 * d:3 * d]
    y_new = y_ref[...] + gate * out
    outs[0][...] = y_new
    if with_next:
        modn_ref, gn_ref, wn_ref = refs[5:8]
        _modulated_projection(y_new, modn_ref, gn_ref, wn_ref, outs[1], outs[2:], d)


def _outproj(u, r_arr, r_col_block, mod, w, y, seq_len, nxt=None):
    m, kdim = u.shape
    d = y.shape[1]
    tm = min(ROW_TILE if nxt is None else ROW_TILE // 2, seq_len)
    tiles_per_seq = seq_len // tm
    in_specs = [
        pl.BlockSpec((tm, kdim), lambda i: (i, 0)),
        pl.BlockSpec((tm, kdim), lambda i: (i, r_col_block)),
        pl.BlockSpec((1, 1, 3 * d), _mod_map(mod, tiles_per_seq)),
        pl.BlockSpec((kdim, d), lambda i: (0, 0)),
        pl.BlockSpec((tm, d), lambda i: (i, 0)),
    ]
    args = [u, r_arr, mod, w, y]
    out_specs = [pl.BlockSpec((tm, d), lambda i: (i, 0))]
    out_shape = [jax.ShapeDtypeStruct((m, d), F32)]
    if nxt is not None:
        mod_n, g_n, w_n, out_dtype, n_tail = nxt
        in_specs += [
            pl.BlockSpec((1, 1, 3 * d), _mod_map(mod_n, tiles_per_seq)),
            pl.BlockSpec((1, d), lambda i: (0, 0)),
            pl.BlockSpec(w_n.shape, lambda i: (0, 0)),
        ]
        args += [mod_n, g_n, w_n]
        specs_n, shapes_n = _proj_out_specs(m, tm, w_n.shape[1], out_dtype, n_tail)
        out_specs += specs_n
        out_shape += shapes_n
    return pl.pallas_call(
        functools.partial(_outproj_kernel, d=d, with_next=nxt is not None),
        grid=(m // tm,),
        in_specs=in_specs,
        out_specs=out_specs,
        out_shape=out_shape,
        compiler_params=_params(("parallel",)),
        name="outproj",
    )(*args)


def _mlstm_pre_kernel(xm_ref, prev_ref, next_ref, cw_ref, cb_ref, bdq_ref, bdk_ref, bdv_ref, wg_ref, bg_ref,
                      q_ref, k_ref, v_ref, xc_ref, gc_ref, gr_ref, *, tiles_per_seq, n_heads, q_scale):
    i = pl.program_id(0)
    tm, inner = xm_ref.shape
    xm = xm_ref[...].astype(F32)
    row = lax.broadcasted_iota(jnp.int32, (tm, 1), 0)
    first = (i % tiles_per_seq) == 0
    last = (i % tiles_per_seq) == tiles_per_seq - 1
    halo = prev_ref.shape[0]
    prev_row = jnp.where(first, 0.0, prev_ref[halo - 1:halo, :].astype(F32))
    next_row = jnp.where(last, 0.0, next_ref[0:1, :].astype(F32))
    x_m1 = jnp.where(row == 0, prev_row, pltpu.roll(xm, 1, 0))
    x_p1 = jnp.where(row == tm - 1, next_row, pltpu.roll(xm, tm - 1, 0))
    conv = cw_ref[0:1, :] * x_m1 + cw_ref[1:2, :] * xm + cw_ref[2:3, :] * x_p1 + cb_ref[...]
    xc = _silu(conv)
    xc_ref[...] = xc.astype(xc_ref.dtype)

    xc_b = xc.astype(BF16)
    xm_b = xm.astype(BF16)
    blk = bdq_ref.shape[1]
    acc = jnp.zeros((tm, wg_ref.shape[1]), F32)
    for j in range(inner // blk):
        cols = slice(j * blk, (j + 1) * blk)
        qj = _dot(xc_b[:, cols], bdq_ref[j])
        kj = _dot(xc_b[:, cols], bdk_ref[j])
        vj = _dot(xm_b[:, cols], bdv_ref[j])
        qb, kb, vb = qj.astype(BF16), kj.astype(BF16), vj.astype(BF16)
        q_ref[:, cols] = (qj * q_scale).astype(BF16)
        k_ref[:, cols] = kb
        v_ref[:, cols] = vb
        acc = acc + _dot(qb, wg_ref[cols, :])
        acc = acc + _dot(kb, wg_ref[inner + j * blk:inner + (j + 1) * blk, :])
        acc = acc + _dot(vb, wg_ref[2 * inner + j * blk:2 * inner + (j + 1) * blk, :])
    g = acc + bg_ref[...]
    lane = lax.broadcasted_iota(jnp.int32, g.shape, 1)
    kind = lane % 4
    lf = jnp.where((kind >= 2) & (lane < 4 * n_heads), _log_sigmoid(g), 0.0)
    r = lax.broadcasted_iota(jnp.int32, (tm, tm), 0)
    c = lax.broadcasted_iota(jnp.int32, (tm, tm), 1)
    prefix = _dot_exact_lhs((c <= r).astype(BF16), lf)
    suffix = _dot_exact_lhs((c >= r).astype(BF16), lf)
    gcol = jnp.where(kind < 2, g, jnp.where(kind == 2, prefix, suffix))
    grow = jnp.transpose(gcol)
    for h in range(n_heads):
        gc_ref[h] = gcol[:, 4 * h:4 * h + 4]
        gr_ref[h] = grow[4 * h:4 * h + 4, :]


def _mlstm_pre(xz, conv_w, conv_b, bdq, bdk, bdv, wg, bg, seq_len, n_heads):
    m = xz.shape[0]
    inner = conv_w.shape[1]
    tm = MLSTM_CHUNK
    tiles_per_seq = seq_len // tm
    halo = 16
    last_halo = m // halo - 1
    return pl.pallas_call(
        functools.partial(_mlstm_pre_kernel, tiles_per_seq=tiles_per_seq, n_heads=n_heads,
                          q_scale=(inner // n_heads) ** -0.5),
        grid=(m // tm,),
        in_specs=[
            pl.BlockSpec((tm, inner), lambda i: (i, 0)),
            pl.BlockSpec((halo, inner), lambda i: (jnp.maximum(i * (tm // halo) - 1, 0), 0)),
            pl.BlockSpec((halo, inner), lambda i: (jnp.minimum((i + 1) * (tm // halo), last_halo), 0)),
            pl.BlockSpec(conv_w.shape, lambda i: (0, 0)),
            pl.BlockSpec((1, inner), lambda i: (0, 0)),
            pl.BlockSpec(bdq.shape, lambda i: (0, 0, 0)),
            pl.BlockSpec(bdk.shape, lambda i: (0, 0, 0)),
            pl.BlockSpec(bdv.shape, lambda i: (0, 0, 0)),
            pl.BlockSpec(wg.shape, lambda i: (0, 0)),
            pl.BlockSpec(bg.shape, lambda i: (0, 0)),
        ],
        out_specs=[
            pl.BlockSpec((tm, inner), lambda i: (i, 0)),
            pl.BlockSpec((tm, inner), lambda i: (i, 0)),
            pl.BlockSpec((tm, inner), lambda i: (i, 0)),
            pl.BlockSpec((tm, inner), lambda i: (i, 0)),
            pl.BlockSpec((n_heads, tm, 4), lambda i: (0, i, 0)),
            pl.BlockSpec((n_heads, 4, tm), lambda i: (0, 0, i)),
        ],
        out_shape=[
            jax.ShapeDtypeStruct((m, inner), BF16),
            jax.ShapeDtypeStruct((m, inner), BF16),
            jax.ShapeDtypeStruct((m, inner), BF16),
            jax.ShapeDtypeStruct((m, inner), BF16),
            jax.ShapeDtypeStruct((n_heads, m, 4), F32),
            jax.ShapeDtypeStruct((n_heads, 4, m), F32),
        ],
        compiler_params=_params(("parallel",)),
        name="mlstm_pre",
    )(xz, xz, xz, conv_w, conv_b, bdq, bdk, bdv, wg, bg)


def _mlstm_scan_kernel(*refs, n_chunks, has_state, emit_state, has_alias, state_slot):
    q_ref, k_ref, v_ref, xc_ref, gc_ref, gr_ref, og_ref, skip_ref = refs[:8]
    pos = 8
    if has_state:
        c0_ref, n0_ref, m0_ref = refs[pos:pos + 3]
        pos += 3
    pos += int(has_alias)
    u_ref = refs[pos]
    pos += 1
    if emit_state:
        cn_ref, nn_ref, mn_ref = refs[pos:pos + 3]
        pos += 3
        for other in range(cn_ref.shape[1]):
            if other != state_slot:
                cn_ref[0, other] = jnp.zeros(cn_ref.shape[2:], F32)
    c_scr, n_scr, m_scr, acc_scr = refs[pos:pos + 4]

    L = MLSTM_CHUNK
    dh = q_ref.shape[1]
    row = lax.broadcasted_iota(jnp.int32, (L, L), 0)
    col = lax.broadcasted_iota(jnp.int32, (L, L), 1)

    def gates(rows, d):
        gc = gc_ref[0, rows, :]
        gr = gr_ref[0, :, rows]
        ig_col, b_col = gc[:, d:d + 1], gc[:, 2 + d:3 + d]
        ig_row, b_row = gr[d:d + 1, :], gr[2 + d:3 + d, :]
        mask = (col <= row) if d == 0 else (col >= row)
        dmat = jnp.where(mask, b_col - b_row + ig_row, -jnp.inf)
        b_last = b_col[L - 1:L, :] if d == 0 else b_col[0:1, :]
        return dmat, b_col, b_last - b_col + ig_col, b_last

    def finalize(rows, hs):
        mu = jnp.mean(hs, axis=-1, keepdims=True)
        hc = hs - mu
        hn = hc * lax.rsqrt(jnp.mean(hc * hc, axis=-1, keepdims=True) + EPS) * og_ref[...]
        u_ref[rows, :] = (hn + skip_ref[...] * xc_ref[rows, :].astype(F32)).astype(u_ref.dtype)

    if n_chunks == 1 and not has_state:
        rows = pl.ds(0, L)
        q, k, v = q_ref[...], k_ref[...], v_ref[...]
        s_qk = _dot_nt(q, k)
        p = None
        for d in range(2):
            dmat, b_col, w_state, b_last = gates(rows, d)
            m_t = jnp.max(dmat, axis=1, keepdims=True)
            s = s_qk * jnp.exp(dmat - m_t)
            den = jnp.sum(s, axis=1, keepdims=True)
            s = s / jnp.maximum(jnp.abs(den), jnp.exp(-m_t))
            p = s if p is None else p + s
            if emit_state:
                m_new = jnp.maximum(b_last, jnp.max(w_state, axis=0, keepdims=True))
                kw = k.astype(F32) * jnp.exp(w_state - m_new)
                cn_ref[0, state_slot, d, 0] = _dot_tn(kw.astype(BF16), v)
                nn_ref[0, d, 0] = jnp.sum(kw, axis=0, keepdims=True)
                mn_ref[0, d, 0] = jnp.broadcast_to(m_new, mn_ref.shape[3:])
        finalize(rows, _dot(p.astype(BF16), v))
        return

    assert n_chunks % 2 == 0

    def direction(c, d, finish, update):
        rows = pl.ds(pl.multiple_of(c * L, L), L)
        q = q_ref[rows, :]
        k = k_ref[rows, :]
        v = v_ref[rows, :]
        dmat, b_col, w_state, b_last = gates(rows, d)
        m_prev = m_scr[d]
        c_prev = c_scr[d]
        n_prev = n_scr[d]
        inter = b_col + m_prev
        m_t = jnp.maximum(inter, jnp.max(dmat, axis=1, keepdims=True))
        s = _dot_nt(q, k) * jnp.exp(dmat - m_t)
        w_inter = jnp.exp(inter - m_t)
        num = _dot(s.astype(BF16), v) + w_inter * _dot(q, c_prev.astype(BF16))
        den = (jnp.sum(s, axis=1, keepdims=True)
               + w_inter * jnp.sum(q.astype(F32) * n_prev, axis=1, keepdims=True))
        h = num / jnp.maximum(jnp.abs(den), jnp.exp(-m_t))
        if finish:
            finalize(rows, acc_scr[rows, :] + h)
        else:
            acc_scr[rows, :] = h
        if update:
            m_new = jnp.maximum(b_last + m_prev, jnp.max(w_state, axis=0, keepdims=True))
            decay = jnp.exp(b_last + m_prev - m_new)
            kw = k.astype(F32) * jnp.exp(w_state - m_new)
            c_scr[d] = decay * c_prev + _dot_tn(kw.astype(BF16), v)
            n_scr[d] = decay * n_prev + jnp.sum(kw, axis=0, keepdims=True)
            m_scr[d] = m_new

    def step(s, finish, update):
        direction(s, 0, finish, update)
        direction(n_chunks - 1 - s, 1, finish, update)

    for d in range(2):
        if has_state:
            c_scr[d] = c0_ref[0, 0, d, 0]
            n_scr[d] = n0_ref[0, 0, d, 0]
            m_scr[d] = m0_ref[0, 0, d, 0][:, 0:1]
        else:
            c_scr[d] = jnp.zeros((dh, dh), F32)
            n_scr[d] = jnp.zeros((1, dh), F32)
            m_scr[d] = jnp.zeros((1, 1), F32)

    half = n_chunks // 2

    def first_half(s, carry):
        step(s, False, True)
        return carry

    def second_half(s, carry):
        step(s, True, True)
        return carry

    lax.fori_loop(0, half, first_half, 0)
    lax.fori_loop(half, n_chunks - 1, second_half, 0)
    step(n_chunks - 1, True, emit_state)
    if emit_state:
        for d in range(2):
            cn_ref[0, state_slot, d, 0] = c_scr[d]
            nn_ref[0, d, 0] = n_scr[d]
            mn_ref[0, d, 0] = jnp.broadcast_to(m_scr[d], mn_ref.shape[3:])


def _mlstm_scan(q, k, v, xc, gch, grh, out_g, skip, state, layer_idx, n_layers, c_stack, n_seq, seq_len, n_heads,
                emit_state):
    m, inner = q.shape
    dh = inner // n_heads
    n_chunks = seq_len // MLSTM_CHUNK
    has_state = state is not None
    j = layer_idx
    tok = lambda b, h: (b, h)
    in_specs = [
        pl.BlockSpec((seq_len, dh), tok),
        pl.BlockSpec((seq_len, dh), tok),
        pl.BlockSpec((seq_len, dh), tok),
        pl.BlockSpec((seq_len, dh), tok),
        pl.BlockSpec((1, seq_len, 4), lambda b, h: (h, b, 0)),
        pl.BlockSpec((1, 4, seq_len), lambda b, h: (h, 0, b)),
        pl.BlockSpec((1, dh), lambda b, h: (0, h)),
        pl.BlockSpec((1, dh), lambda b, h: (0, h)),
    ]
    args = [q, k, v, xc, gch, grh, out_g, skip]
    if has_state:
        c0, n0, m0 = state
        in_specs += [
            pl.BlockSpec((1, 1, 2, 1, dh, dh), lambda b, h: (b, j, 0, h, 0, 0)),
            pl.BlockSpec((1, 1, 2, 1, 1, dh), lambda b, h: (b, j, 0, h, 0, 0)),
            pl.BlockSpec((1, 1, 2, 1, 1, 128), lambda b, h: (b, j, 0, h, 0, 0)),
        ]
        args += [c0, n0, m0]
    out_specs = [pl.BlockSpec((seq_len, dh), tok)]
    out_shape = [jax.ShapeDtypeStruct((m, inner), BF16)]
    aliases = {}
    state_slot = 0
    if emit_state:
        if c_stack is None:
            c_spec = pl.BlockSpec((1, n_layers, 2, 1, dh, dh), lambda b, h: (b, 0, 0, h, 0, 0))
            state_slot = j
        else:
            c_spec = pl.BlockSpec((1, 1, 2, 1, dh, dh), lambda b, h: (b, j, 0, h, 0, 0))
        out_specs += [
            c_spec,
            pl.BlockSpec((1, 2, 1, 1, dh), lambda b, h: (b, 0, h, 0, 0)),
            pl.BlockSpec((1, 2, 1, 1, 128), lambda b, h: (b, 0, h, 0, 0)),
        ]
        out_shape += [
            jax.ShapeDtypeStruct((n_seq, n_layers, 2, n_heads, dh, dh), F32),
            jax.ShapeDtypeStruct((n_seq, 2, n_heads, 1, dh), F32),
            jax.ShapeDtypeStruct((n_seq, 2, n_heads, 1, 128), F32),
        ]
        if c_stack is not None:
            in_specs.append(pl.BlockSpec(memory_space=pl.ANY))
            args.append(c_stack)
            aliases = {len(args) - 1: 1}
    return pl.pallas_call(
        functools.partial(_mlstm_scan_kernel, n_chunks=n_chunks, has_state=has_state, emit_state=emit_state,
                          has_alias=bool(aliases), state_slot=state_slot),
        grid=(n_seq, n_heads),
        in_specs=in_specs,
        out_specs=out_specs,
        out_shape=out_shape,
        input_output_aliases=aliases,
        scratch_shapes=[pltpu.VMEM((2, dh, dh), F32), pltpu.VMEM((2, 1, dh), F32), pltpu.VMEM((2, 1, 1), F32),
                        pltpu.VMEM((seq_len if n_chunks > 1 else 8, dh), F32)],
        compiler_params=_params(("parallel", "parallel")),
        name="mlstm_scan",
    )(*args)


def _gla_scan_kernel(*refs, n_chunks, heads, has_state, emit_state):
    q_ref, k_ref, v_ref, a_ref, w2_ref, ba_ref, og_ref, dsel_ref = refs[:8]
    pos = 8
    if has_state:
        s0_ref = refs[pos]
        pos += 1
    o_ref = refs[pos]
    pos += 1
    if emit_state:
        sn_ref = refs[pos]
        pos += 1
    st_scr, acc_scr = refs[pos:pos + 2]

    L = GLA_CHUNK
    DB = GLA_DIAG
    dk = q_ref.shape[1] // heads
    dv = v_ref.shape[1] // heads
    qscale = dk ** -0.5
    row = lax.broadcasted_iota(jnp.int32, (L, L), 0)
    col = lax.broadcasted_iota(jnp.int32, (L, L), 1)
    sub = lax.broadcasted_iota(jnp.int32, (L // DB, DB, 1), 1)
    same_diag = (row // DB) == (col // DB)

    def direction(c, d, finish, update):
        rev = d == 1
        rows = pl.ds(pl.multiple_of(c * L, L), L)
        la = _log_sigmoid(_dot(a_ref[rows, :].astype(BF16), w2_ref[d]) + ba_ref[d]) * (LOG2E / GLA_TAU)
        tri = jnp.where((col >= row) if rev else (col <= row), 1.0, 0.0).astype(BF16)
        b_all = _dot_exact_lhs(tri, la)
        for hh in range(heads):
            kcols = slice(hh * dk, (hh + 1) * dk)
            vcols = slice(hh * dv, (hh + 1) * dv)
            q = q_ref[rows, kcols].astype(F32) * qscale
            k = k_ref[rows, kcols].astype(F32)
            v = v_ref[rows, vcols].astype(BF16)
            b = b_all[:, kcols]
            st = st_scr[d, hh]

            o = _dot_nt((q * jnp.exp2(b)).astype(BF16), st.astype(BF16))

            att = jnp.zeros((L, L), F32)
            blk = L // 2
            while blk >= DB:
                pieces = []
                for g in range(L // (2 * blk)):
                    idx = g * 2 * blk + (blk if rev else blk - 1)
                    pieces.append(jnp.broadcast_to(b[idx:idx + 1, :], (2 * blk, dk)))
                ref_rows = pieces[0] if len(pieces) == 1 else jnp.concatenate(pieces, axis=0)
                e = jnp.exp2(-jnp.abs(b - ref_rows))
                a_lvl = _dot_nt((q * e).astype(BF16), (k * e).astype(BF16))
                rb, cb = row // blk, col // blk
                if rev:
                    pair = ((rb % 2) == 0) & (cb == rb + 1)
                else:
                    pair = ((rb % 2) == 1) & (cb == rb - 1)
                att = jnp.where(pair, a_lvl, att)
                blk //= 2

            q3 = q.reshape(L // DB, DB, dk)
            k3 = k.reshape(L // DB, DB, dk)
            b3 = b.reshape(L // DB, DB, dk)
            parts = []
            for j in range(DB):
                valid = (sub <= j) if rev else (sub >= j)
                x = q3 * k3[:, j:j + 1, :] * jnp.exp2(jnp.where(valid, b3 - b3[:, j:j + 1, :], -jnp.inf))
                parts.append(x.reshape(L, dk).astype(BF16))
            diag = _dot(jnp.concatenate(parts, axis=1), dsel_ref[...])
            att = jnp.where(same_diag, diag, att)

            o = o + _dot(att.astype(BF16), v)
            if not finish:
                acc_scr[rows, vcols] = o
            else:
                os_ = acc_scr[rows, vcols] + o
                o_ref[rows, vcols] = (os_ * lax.rsqrt(jnp.mean(os_ * os_, axis=-1, keepdims=True) + EPS)
                                      * og_ref[:, vcols]).astype(o_ref.dtype)
            if update:
                b_last = b[0:1, :] if rev else b[L - 1:L, :]
                kt = (k * jnp.exp2(b_last - b)).astype(BF16)
                st_scr[d, hh] = st * jnp.exp2(b_last) + _dot_tn(v, kt)

    def step(s, finish, update):
        direction(s, 0, finish, update)
        direction(n_chunks - 1 - s, 1, finish, update)

    for d in range(2):
        for hh in range(heads):
            if has_state:
                st_scr[d, hh] = jnp.transpose(s0_ref[0, 0, d, hh])
            else:
                st_scr[d, hh] = jnp.zeros((dv, dk), F32)

    half = n_chunks // 2

    def first_half(s, carry):
        step(s, False, True)
        return carry

    def second_half(s, carry):
        step(s, True, True)
        return carry

    lax.fori_loop(0, half, first_half, 0)
    lax.fori_loop(half, n_chunks - 1, second_half, 0)
    step(n_chunks - 1, True, emit_state)
    if emit_state:
        for d in range(2):
            for hh in range(heads):
                sn_ref[0, 0, d, hh] = jnp.transpose(st_scr[d, hh])


def _gla_scan(proj, aproj, w2pad, b_a, out_g, dsel, state, n_seq, seq_len, n_heads, dk_total, dv_total, emit_state):
    m = proj.shape[0]
    heads = GLA_HEADS_PER_STEP
    dk = dk_total // n_heads
    dv = dv_total // n_heads
    n_chunks = seq_len // GLA_CHUNK
    assert n_chunks % 2 == 0 and n_heads % heads == 0
    has_state = state is not None
    wk, wv = heads * dk, heads * dv
    in_specs = [
        pl.BlockSpec((seq_len, wk), lambda b, h: (b, h)),
        pl.BlockSpec((seq_len, wk), lambda b, h: (b, dk_total // wk + h)),
        pl.BlockSpec((seq_len, wv), lambda b, h: (b, 2 * dk_total // wv + h)),
        pl.BlockSpec((seq_len, 128), lambda b, h: (b, 0)),
        pl.BlockSpec((2, 128, wk), lambda b, h: (0, 0, h)),
        pl.BlockSpec((2, 1, wk), lambda b, h: (0, 0, h)),
        pl.BlockSpec((1, wv), lambda b, h: (0, h)),
        pl.BlockSpec(dsel.shape, lambda b, h: (0, 0)),
    ]
    args = [proj, proj, proj, aproj, w2pad, b_a, out_g, dsel]
    if has_state:
        in_specs.append(pl.BlockSpec((1, 1, 2, heads, dk, dv), lambda b, h: (b, 0, 0, h, 0, 0)))
        args.append(state)
    out_specs = [pl.BlockSpec((seq_len, wv), lambda b, h: (b, h))]
    out_shape = [jax.ShapeDtypeStruct((m, dv_total), BF16)]
    if emit_state:
        out_specs.append(pl.BlockSpec((1, 1, 2, heads, dk, dv), lambda b, h: (b, 0, 0, h, 0, 0)))
        out_shape.append(jax.ShapeDtypeStruct((n_seq, 1, 2, n_heads, dk, dv), F32))
    return pl.pallas_call(
        functools.partial(_gla_scan_kernel, n_chunks=n_chunks, heads=heads, has_state=has_state,
                          emit_state=emit_state),
        grid=(n_seq, n_heads // heads),
        in_specs=in_specs,
        out_specs=out_specs,
        out_shape=out_shape,
        scratch_shapes=[pltpu.VMEM((2, heads, dv, dk), F32), pltpu.VMEM((seq_len, wv), F32)],
        compiler_params=_params(("parallel", "parallel")),
        name="gla_scan",
    )(*args)


def _pair_rmsnorm(x, g, lane_lo):
    w = x.shape[-1]
    r = lax.broadcasted_iota(jnp.int32, (w, w), 0) < w // 2
    c = lax.broadcasted_iota(jnp.int32, (w, w), 1) < w // 2
    same_half = jnp.where(r == c, 1.0, 0.0).astype(BF16)
    x2 = x * x
    hi = x2.astype(BF16)
    lo = (x2 - hi.astype(F32)).astype(BF16)
    ms = (_dot(hi, same_half) + _dot(lo, same_half)) * (2.0 / w)
    return x * lax.rsqrt(ms + EPS) * g


def _na_ctx_kernel(q_ref, k_ref, v_ref, qg_ref, kg_ref, o_ref, kn_ref, vn_ref):
    w = qg_ref.shape[1]
    dh = w // 2
    scale = dh ** -0.5
    lane_lo = lax.broadcasted_iota(jnp.int32, (1, w), 1) < dh
    for pair in range(q_ref.shape[1] // w):
        cols = slice(pair * w, (pair + 1) * w)
        qn = _pair_rmsnorm(q_ref[:, cols].astype(F32), qg_ref[...], lane_lo)
        kn = _pair_rmsnorm(k_ref[:, cols].astype(F32), kg_ref[...], lane_lo)
        v = v_ref[:, cols].astype(F32)
        for hh in range(2):
            kn_ref[0, 0, 2 * pair + hh] = kn[:, hh * dh:(hh + 1) * dh]
            vn_ref[0, 0, 2 * pair + hh] = v[:, hh * dh:(hh + 1) * dh]
        kb = kn.astype(BF16)
        vb = v.astype(BF16)
        outs = []
        for lo in (True, False):
            sel = lane_lo if lo else jnp.logical_not(lane_lo)
            qh = jnp.where(sel, qn, 0.0).astype(BF16)
            s = _dot_nt(qh, kb) * scale
            p = jnp.exp(s - jnp.max(s, axis=-1, keepdims=True))
            l = jnp.sum(p, axis=-1, keepdims=True)
            outs.append(_dot(p.astype(BF16), vb) / l)
        o_ref[:, cols] = jnp.where(lane_lo, outs[0], outs[1]).astype(o_ref.dtype)


def _na_context(proj, q_g2, k_g2, n_seq, seq_len, d_model):
    m = proj.shape[0]
    dh = q_g2.shape[1] // 2
    wb = NA_CTX_PAIRS * 2 * dh
    nb = d_model // wb
    return pl.pallas_call(
        _na_ctx_kernel,
        grid=(n_seq, nb),
        in_specs=[
            pl.BlockSpec((seq_len, wb), lambda b, p: (b, p)),
            pl.BlockSpec((seq_len, wb), lambda b, p: (b, nb + p)),
            pl.BlockSpec((seq_len, wb), lambda b, p: (b, 2 * nb + p)),
            pl.BlockSpec((1, 2 * dh), lambda b, p: (0, 0)),
            pl.BlockSpec((1, 2 * dh), lambda b, p: (0, 0)),
        ],
        out_specs=[
            pl.BlockSpec((seq_len, wb), lambda b, p: (b, p)),
            pl.BlockSpec((1, 1, 2 * NA_CTX_PAIRS, seq_len, dh), lambda b, p: (b, 0, p, 0, 0)),
            pl.BlockSpec((1, 1, 2 * NA_CTX_PAIRS, seq_len, dh), lambda b, p: (b, 0, p, 0, 0)),
        ],
        out_shape=[jax.ShapeDtypeStruct((m, d_model), BF16),
                   jax.ShapeDtypeStruct((n_seq, 1, d_model // dh, seq_len, dh), F32),
                   jax.ShapeDtypeStruct((n_seq, 1, d_model // dh, seq_len, dh), F32)],
        compiler_params=_params(("parallel", "parallel")),
        name="na_context",
    )(proj, proj, proj, q_g2, k_g2)


def _na_latent_kernel(q_ref, k_ref, v_ref, kc_ref, vc_ref, bias_ref, qg_ref, kg_ref, o_ref, qn_scr, kn_scr, vb_scr,
                      kc_scr, vc_scr, s_scr, m_scr, p_scr, l_scr, *, n_rows, win_h):
    t, w = q_ref.shape
    dh = w // 2
    gw = t // n_rows
    lane_lo = lax.broadcasted_iota(jnp.int32, (1, w), 1) < dh
    qn_scr[...] = _pair_rmsnorm(q_ref[...].astype(F32), qg_ref[...], lane_lo) * (dh ** -0.5 * LOG2E)
    kn_scr[...] = _pair_rmsnorm(k_ref[...].astype(F32), kg_ref[...], lane_lo).astype(BF16)
    vb_scr[...] = v_ref[...].astype(BF16)
    kc_scr[...] = jnp.concatenate([kc_ref[0, 0], kc_ref[0, 1]], axis=1).astype(BF16)
    vc_scr[...] = jnp.concatenate([vc_ref[0, 0], vc_ref[0, 1]], axis=1).astype(BF16)
    n_loc = win_h * gw

    def window(r):
        r0 = jnp.clip(r - win_h // 2, 0, n_rows - win_h)
        return r0, pl.ds(pl.multiple_of(r0 * gw, gw), n_loc)

    def scores(r, slot):
        r0, krows = window(r)
        qr = qn_scr[pl.ds(pl.multiple_of(r * gw, gw), gw), :]
        q2 = jnp.concatenate([jnp.where(lane_lo, qr, 0.0), jnp.where(lane_lo, 0.0, qr)], axis=0).astype(BF16)
        shift = NA_WIN_H - 1 - (r - r0)
        start = pl.multiple_of((shift // 2) * (2 * gw), 2 * gw)
        s_loc = _dot_nt(q2, kn_scr[krows, :]) + bias_ref[shift % 2, 0, :, pl.ds(start, n_loc)]
        s_ctx = _dot_nt(q2, kc_scr[...])
        s_scr[slot, :, 0:n_loc] = s_loc
        s_scr[slot, :, n_loc:] = s_ctx
        m_scr[slot] = jnp.maximum(jnp.max(s_loc, axis=-1, keepdims=True), jnp.max(s_ctx, axis=-1, keepdims=True))

    def probabilities(slot):
        p = jnp.exp2(s_scr[slot] - m_scr[slot])
        l_scr[slot] = jnp.sum(p, axis=-1, keepdims=True)
        p_scr[slot] = p.astype(BF16)

    def output(r, slot):
        _, krows = window(r)
        o2 = (_dot(p_scr[slot, :, 0:n_loc], vb_scr[krows, :]) + _dot(p_scr[slot, :, n_loc:], vc_scr[...])) / l_scr[slot]
        o_ref[pl.ds(pl.multiple_of(r * gw, gw), gw), :] = jnp.where(lane_lo, o2[0:gw], o2[gw:]).astype(o_ref.dtype)

    scores(0, 0)
    scores(1, 1)
    probabilities(0)

    def body(ii, carry):
        for slot in range(2):
            i = 2 * ii + slot
            output(i, slot)
            probabilities(1 - slot)
            scores(i + 2, slot)
        return carry

    lax.fori_loop(0, n_rows // 2 - 1, body, 0)
    output(n_rows - 2, 0)
    probabilities(1)
    output(n_rows - 1, 1)


def _na_latent(proj, kc, vc, bias, q_g2, k_g2, n_seq, seq_len, d_model):
    m = proj.shape[0]
    nb = d_model // 128
    n_rows = seq_len // GRID_W
    win_h = min(NA_WIN_H, n_rows)
    past, dh = kc.shape[2:]
    n_keys = win_h * GRID_W + past
    assert n_rows % 2 == 0 and n_rows >= 4
    return pl.pallas_call(
        functools.partial(_na_latent_kernel, n_rows=n_rows, win_h=win_h),
        grid=(n_seq, nb),
        in_specs=[
            pl.BlockSpec((seq_len, 128), lambda b, p: (b, p)),
            pl.BlockSpec((seq_len, 128), lambda b, p: (b, nb + p)),
            pl.BlockSpec((seq_len, 128), lambda b, p: (b, 2 * nb + p)),
            pl.BlockSpec((1, 2, past, dh), lambda b, p: (b, p, 0, 0)),
            pl.BlockSpec((1, 2, past, dh), lambda b, p: (b, p, 0, 0)),
            pl.BlockSpec((2, 1) + bias.shape[2:], lambda b, p: (0, p, 0, 0)),
            pl.BlockSpec((1, 128), lambda b, p: (0, 0)),
            pl.BlockSpec((1, 128), lambda b, p: (0, 0)),
        ],
        out_specs=pl.BlockSpec((seq_len, 128), lambda b, p: (b, p)),
        out_shape=jax.ShapeDtypeStruct((m, d_model), BF16),
        scratch_shapes=[pltpu.VMEM((seq_len, 128), F32), pltpu.VMEM((seq_len, 128), BF16),
                        pltpu.VMEM((seq_len, 128), BF16), pltpu.VMEM((past, 128), BF16),
                        pltpu.VMEM((past, 128), BF16), pltpu.VMEM((2, 2 * GRID_W, n_keys), F32),
                        pltpu.VMEM((2, 2 * GRID_W, 1), F32), pltpu.VMEM((2, 2 * GRID_W, n_keys), BF16),
                        pltpu.VMEM((2, 2 * GRID_W, 1), F32)],
        compiler_params=_params(("parallel", "parallel")),
        name="na_latent",
    )(proj, proj, proj, kc, vc, bias, q_g2, k_g2)


def _na_bias_table(rpb, n_rows):
    win_h = min(NA_WIN_H, n_rows)
    n_h, n_rr, n_cr = rpb.shape
    cols = jnp.arange(GRID_W)
    col_start = jnp.clip(cols - NA_WIN_W // 2, 0, GRID_W - NA_WIN_W)
    in_win = (cols[None, :] >= col_start[:, None]) & (cols[None, :] < col_start[:, None] + NA_WIN_W)
    period = GRID_W + 1
    u = jnp.concatenate([rpb[:, :, NA_WIN_W - 1:], jnp.zeros((n_h, n_rr, period - n_cr), rpb.dtype),
                         rpb[:, :, :NA_WIN_W - 1]], axis=2)
    toe = jnp.tile(u, (1, 1, GRID_W))[:, :, :GRID_W * GRID_W].reshape(n_h, n_rr, GRID_W, GRID_W)
    toe = toe.transpose(0, 2, 1, 3)
    toe = jnp.where(in_win[None, :, None, :], toe * LOG2E, -jnp.inf).astype(F32)
    wide = toe.reshape(n_h // 2, 2 * GRID_W, n_rr * GRID_W)
    lanes = (NA_WIN_H - 1 + win_h + 1) // 2 * 2 * GRID_W
    pad = lambda a: jnp.pad(a, ((0, 0), (0, 0), (0, lanes - a.shape[2])))
    return jnp.stack([pad(wide), pad(wide[:, :, GRID_W:])], axis=0)


def _block_diag_tiles(w, tile):
    n_blk, s, _ = w.shape
    per = tile // s
    n_tiles = n_blk // per
    rows = w.reshape(n_tiles, per, s, s).transpose(0, 2, 1, 3).reshape(n_tiles, 1, s, tile)
    full = jnp.broadcast_to(rows, (n_tiles, per, s, tile)).reshape(n_tiles, tile, tile)
    blk = jnp.arange(tile) // s
    return jnp.where((blk[:, None] == blk[None, :])[None], full, 0.0)


def _mlstm_layer(streams, projs, p, state, j, n_layers, c_stack):
    (conv_w, conv_b, wq, wk, wv, w_ig, b_ig, w_fg, b_fg, out_g, skip, w_out) = p
    inner = conv_w.shape[1]
    n_heads = b_ig.shape[1]
    w_out_b = w_out.astype(BF16)
    bdq = _block_diag_tiles(wq, 256).astype(BF16)
    bdk = _block_diag_tiles(wk, 256).astype(BF16)
    bdv = _block_diag_tiles(wv, 256).astype(BF16)
    n_g = 4 * n_heads
    wg = jnp.stack([w_ig[0], w_ig[1], w_fg[0], w_fg[1]], axis=2).reshape(3 * inner, n_g)
    wg = jnp.pad(wg, ((0, 0), (0, 128 - n_g))).astype(BF16)
    bg = jnp.pad(jnp.stack([b_ig[0], b_ig[1], b_fg[0], b_fg[1]], axis=1).reshape(n_g), (0, 128 - n_g)).reshape(1, 128)
    outs, new_state = [], None
    for (n_seq, seq_len, is_prompt), (xz,) in zip(streams, projs):
        q, k, v, xc, gch, grh = _mlstm_pre(xz, conv_w, conv_b.reshape(1, inner), bdq, bdk, bdv, wg, bg, seq_len,
                                           n_heads)
        st = None
        if not is_prompt:
            c0, n0, m0 = state
            st = (c0, n0.reshape(n0.shape[:4] + (1, n0.shape[4])),
                  jnp.broadcast_to(m0[..., None, None], m0.shape + (1, 128)))
        res = _mlstm_scan(q, k, v, xc, gch, grh, out_g.reshape(1, inner), skip.reshape(1, inner), st, j, n_layers,
                          c_stack, n_seq, seq_len, n_heads, emit_state=is_prompt)
        if is_prompt:
            u, cn, nn, mn = res
            new_state = (cn, nn[:, :, :, 0, :], mn[:, :, :, 0, 0])
        else:
            u = res[0]
        outs.append((u, xz, 1))
    return outs, new_state, w_out_b


def _gla_in_weights(w_in, w_a1):
    pad = 128 - 2 * GLA_RANK
    return jnp.concatenate([w_in, w_a1[0], w_a1[1], jnp.zeros((w_in.shape[0], pad), F32)], axis=1).astype(BF16)


def _gla_layer(streams, projs, p, state):
    (w_a2, b_a, out_g, w_out) = p
    n_heads = state.shape[3]
    dk_total = w_a2.shape[2]
    dv_total = out_g.shape[0]
    w2pad = jnp.zeros((2, 128, dk_total), F32)
    w2pad = w2pad.at[0, 0:GLA_RANK].set(w_a2[0]).at[1, GLA_RANK:2 * GLA_RANK].set(w_a2[1]).astype(BF16)
    w_out_b = w_out.astype(BF16)
    L, DB = GLA_CHUNK, GLA_DIAG
    dk = dk_total // n_heads
    dsel = (jnp.arange(DB * dk)[:, None] // dk == jnp.arange(L)[None, :] % DB).astype(BF16)
    outs, new_state = [], None
    for (n_seq, seq_len, is_prompt), (proj, aproj) in zip(streams, projs):
        res = _gla_scan(proj, aproj, w2pad, b_a.reshape(2, 1, dk_total), out_g.reshape(1, dv_total), dsel,
                        None if is_prompt else state, n_seq, seq_len, n_heads, dk_total, dv_total,
                        emit_state=is_prompt)
        if is_prompt:
            new_state = res[1]
        outs.append((res[0], proj, (2 * dk_total + dv_total) // dv_total))
    return outs, new_state, w_out_b


def _na_layer(streams, projs, p, cache_k, cache_v):
    (q_g, k_g, rpb, w_out) = p
    d = w_out.shape[0]
    n_heads = rpb.shape[0]
    dh = d // n_heads
    w_out_b = w_out.astype(BF16)
    q_g2 = jnp.tile(q_g, 2).reshape(1, 2 * dh)
    k_g2 = jnp.tile(k_g, 2).reshape(1, 2 * dh)
    outs, new_kv = [], None
    for (n_seq, seq_len, is_prompt), (proj,) in zip(streams, projs):
        if is_prompt:
            o, kn, vn = _na_context(proj, q_g2, k_g2, n_seq, seq_len, d)
            new_kv = (kn, vn)
        else:
            bias = _na_bias_table(rpb, seq_len // GRID_W)
            o = _na_latent(proj, cache_k, cache_v, bias, q_g2, k_g2, n_seq, seq_len, d)
        outs.append((o, proj, 3))
    return outs, new_kv, w_out_b


def kernel(x_prompt, x_sample, state_mlstm_C, state_mlstm_n, state_mlstm_m, state_gla_S, cache_na_k, cache_na_v, c, c_ctx, norm_g, w_mod, b_mod, a_w_in, a_conv_w, a_conv_b, a_wq, a_wk, a_wv, a_w_ig, a_b_ig, a_w_fg, a_b_fg, a_out_g, a_skip, a_w_out, b_w_in, b_w_a1, b_w_a2, b_b_a, b_out_g, b_w_out, c_w_in, c_q_g, c_k_g, c_rpb, c_w_out):
    bp, seq, d = x_prompt.shape
    bs, dec_seq, _ = x_sample.shape
    depth = w_mod.shape[0]
    n_mlstm = state_mlstm_C.shape[1]

    cs = jnp.zeros((16, d), F32).at[0:bs].set(c).at[bs].set(c_ctx)
    mod_all = _modulation(cs, w_mod, b_mod)

    streams = [(bp, seq, True), (bs, dec_seq, False)]
    ys_ = [x_prompt.reshape(bp * seq, d), x_sample.reshape(bs * dec_seq, d)]

    def mods(l):
        return [mod_all[l, bs:bs + 1].reshape(1, 1, 3 * d), mod_all[l, 0:bs].reshape(bs, 1, 3 * d)]

    in_weights = {}

    def in_projection(l, is_prompt):
        kind, j = l % 3, l // 3
        if l not in in_weights:
            in_weights[l] = (_gla_in_weights(b_w_in[j], b_w_a1[j]) if kind == 1
                             else (a_w_in, c_w_in)[kind // 2][j].astype(BF16))
        out_dtype = F32 if (kind == 2 and is_prompt) else BF16
        return in_weights[l], out_dtype, 128 if kind == 1 else 0

    projs = []
    for i, (n_seq, seq_len, is_prompt) in enumerate(streams):
        w0, dt0, tail0 = in_projection(0, is_prompt)
        projs.append(_inproj(ys_[i], mods(0)[i], norm_g[0].reshape(1, d), w0, seq_len, dt0, tail0))

    c_stack = None
    new_n, new_m, new_S, new_k, new_v = [], [], [], [], []
    for l in range(depth):
        kind, j = l % 3, l // 3
        if kind == 0:
            pa = (a_conv_w[j], a_conv_b[j], a_wq[j], a_wk[j], a_wv[j], a_w_ig[j], a_b_ig[j],
                  a_w_fg[j], a_b_fg[j], a_out_g[j], a_skip[j], a_w_out[j])
            outs, (c_stack, nn, mn), w_out_b = _mlstm_layer(
                streams, projs, pa, (state_mlstm_C, state_mlstm_n, state_mlstm_m), j, n_mlstm, c_stack)
            new_n.append(nn)
            new_m.append(mn)
        elif kind == 1:
            pb = (b_w_a2[j], b_b_a[j], b_out_g[j], b_w_out[j])
            outs, sn, w_out_b = _gla_layer(streams, projs, pb, state_gla_S[:, j:j + 1])
            new_S.append(sn)
        else:
            pc = (c_q_g[j], c_k_g[j], c_rpb[j], c_w_out[j])
            outs, (kn, vn), w_out_b = _na_layer(streams, projs, pc, cache_na_k[:, j], cache_na_v[:, j])
            new_k.append(kn)
            new_v.append(vn)
        projs = []
        for i, ((n_seq, seq_len, is_prompt), (u, r_arr, r_blk)) in enumerate(zip(streams, outs)):
            nxt = None
            if l + 1 < depth:
                w_n, dt_n, tail_n = in_projection(l + 1, is_prompt)
                nxt = (mods(l + 1)[i], norm_g[l + 1].reshape(1, d), w_n, dt_n, tail_n)
            res = _outproj(u, r_arr, r_blk, mods(l)[i], w_out_b, ys_[i], seq_len, nxt)
            ys_[i] = res[0]
            projs.append(tuple(res[1:]))
    yp, ys = ys_
    return (yp.reshape(bp, seq, d), ys.reshape(bs, dec_seq, d), c_stack, jnp.stack(new_n, axis=1),
            jnp.stack(new_m, axis=1), jnp.concatenate(new_S, axis=1), jnp.concatenate(new_k, axis=1),
            jnp.concatenate(new_v, axis=1))
```

```python
import functools

import jax
import jax.numpy as jnp
from jax import lax
from jax.experimental import pallas as pl
from jax.experimental.pallas import tpu as pltpu

F32 = jnp.float32
BF16 = jnp.bfloat16
EPS = 1e-6

GRID_W = 64
NA_WIN_H = 8
NA_WIN_W = 16
GLA_TAU = 16.0
GLA_RANK = 16
MLSTM_CHUNK = 256
GLA_CHUNK = 64
GLA_DIAG = 8
GLA_HEADS_PER_STEP = 4
NA_CTX_PAIRS = 4
LOG2E = 1.4426950408889634
ROW_TILE = 512
V7X_VMEM_LIMIT = 56 * 1024 * 1024


def _params(sem):
    return pltpu.CompilerParams(dimension_semantics=sem, vmem_limit_bytes=V7X_VMEM_LIMIT)


def _dot(a, b):
    return jnp.dot(a, b, preferred_element_type=F32)


def _dot_nt(a, b):
    return lax.dot_general(a, b, (((1,), (1,)), ((), ())), preferred_element_type=F32)


def _dot_tn(a, b):
    return lax.dot_general(a, b, (((0,), (0,)), ((), ())), preferred_element_type=F32)


def _split3(x):
    hi = x.astype(BF16)
    r1 = x - hi.astype(F32)
    mid = r1.astype(BF16)
    lo = (r1 - mid.astype(F32)).astype(BF16)
    return hi, mid, lo


def _dot_exact_lhs(sel, x):
    hi, mid, lo = _split3(x)
    return _dot(sel, hi) + (_dot(sel, mid) + _dot(sel, lo))


def _silu(x):
    return x * jax.nn.sigmoid(x)


def _log_sigmoid(x):
    return jnp.minimum(x, 0.0) - jnp.log(1.0 + jnp.exp(-jnp.abs(x)))


def _mod_kernel(c_ref, w_ref, b_ref, o_ref):
    a = _silu(c_ref[...])
    w = w_ref[0]
    ah = a.astype(BF16)
    al = (a - ah.astype(F32)).astype(BF16)
    wh = w.astype(BF16)
    wl = (w - wh.astype(F32)).astype(BF16)
    o_ref[0] = _dot(ah, wh) + (_dot(al, wh) + _dot(ah, wl)) + b_ref[0]


def _modulation(cs, w_mod, b_mod):
    depth, d, d3 = w_mod.shape
    rows = cs.shape[0]
    return pl.pallas_call(
        _mod_kernel,
        grid=(depth, d3 // d),
        in_specs=[
            pl.BlockSpec((rows, d), lambda l, j: (0, 0)),
            pl.BlockSpec((1, d, d), lambda l, j: (l, 0, j)),
            pl.BlockSpec((1, 1, d), lambda l, j: (l, 0, j)),
        ],
        out_specs=pl.BlockSpec((1, rows, d), lambda l, j: (l, 0, j)),
        out_shape=jax.ShapeDtypeStruct((depth, rows, d3), F32),
        compiler_params=_params(("parallel", "parallel")),
        name="modulation",
    )(cs, w_mod, b_mod.reshape(depth, 1, d3))


def _modulated_projection(x, mod_ref, g_ref, w_ref, o_ref, tail_ref, d):
    y = x * lax.rsqrt(jnp.mean(x * x, axis=-1, keepdims=True) + EPS) * g_ref[...]
    shift = mod_ref[0, :, 0:d]
    scale = mod_ref[0, :, d:2 * d]
    h = y * (1.0 + scale) + shift
    res = _dot(h.astype(BF16), w_ref[...])
    n_main = o_ref.shape[1]
    o_ref[...] = res[:, 0:n_main].astype(o_ref.dtype)
    if tail_ref:
        tail_ref[0][...] = res[:, n_main:]


def _inproj_kernel(x_ref, mod_ref, g_ref, w_ref, o_ref, *tail_ref, d):
    _modulated_projection(x_ref[...], mod_ref, g_ref, w_ref, o_ref, tail_ref, d)


def _proj_out_specs(m, tm, n, out_dtype, n_tail):
    specs = [pl.BlockSpec((tm, n - n_tail), lambda i: (i, 0))]
    shapes = [jax.ShapeDtypeStruct((m, n - n_tail), out_dtype)]
    if n_tail:
        specs.append(pl.BlockSpec((tm, n_tail), lambda i: (i, 0)))
        shapes.append(jax.ShapeDtypeStruct((m, n_tail), F32))
    return specs, shapes


def _mod_map(mod, tiles_per_seq):
    if mod.shape[0] == 1:
        return lambda i: (0, 0, 0)
    return lambda i: (i // tiles_per_seq, 0, 0)


def _inproj(x, mod, g, w, seq_len, out_dtype=BF16, n_tail=0):
    m, d = x.shape
    n = w.shape[1]
    tm = min(ROW_TILE, seq_len)
    out_specs, out_shape = _proj_out_specs(m, tm, n, out_dtype, n_tail)
    return pl.pallas_call(
        functools.partial(_inproj_kernel, d=d),
        grid=(m // tm,),
        in_specs=[
            pl.BlockSpec((tm, d), lambda i: (i, 0)),
            pl.BlockSpec((1, 1, 3 * d), _mod_map(mod, seq_len // tm)),
            pl.BlockSpec((1, d), lambda i: (0, 0)),
            pl.BlockSpec((d, n), lambda i: (0, 0)),
        ],
        out_specs=out_specs,
        out_shape=out_shape,
        compiler_params=_params(("parallel",)),
        name="inproj",
    )(x, mod, g, w)


def _outproj_kernel(*refs, d, with_next):
    u_ref, r_ref, mod_ref, w_ref, y_ref = refs[:5]
    outs = refs[8:] if with_next else refs[5:]
    a = u_ref[...].astype(F32) * _silu(r_ref[...].astype(F32))
    out = _dot(a.astype(BF16), w_ref[...])
    gate = mod_ref[0, :, 2 * d:3 * d]
    y_new = y_ref[...] + gate * out
    outs[0][...] = y_new
    if with_next:
        modn_ref, gn_ref, wn_ref = refs[5:8]
        _modulated_projection(y_new, modn_ref, gn_ref, wn_ref, outs[1], outs[2:], d)


def _outproj(u, r_arr, r_col_block, mod, w, y, seq_len, nxt=None):
    m, kdim = u.shape
    d = y.shape[1]
    tm = min(ROW_TILE if nxt is None else ROW_TILE // 2, seq_len)
    tiles_per_seq = seq_len // tm
    in_specs = [
        pl.BlockSpec((tm, kdim), lambda i: (i, 0)),
        pl.BlockSpec((tm, kdim), lambda i: (i, r_col_block)),
        pl.BlockSpec((1, 1, 3 * d), _mod_map(mod, tiles_per_seq)),
        pl.BlockSpec((kdim, d), lambda i: (0, 0)),
        pl.BlockSpec((tm, d), lambda i: (i, 0)),
    ]
    args = [u, r_arr, mod, w, y]
    out_specs = [pl.BlockSpec((tm, d), lambda i: (i, 0))]
    out_shape = [jax.ShapeDtypeStruct((m, d), F32)]
    if nxt is not None:
        mod_n, g_n, w_n, out_dtype, n_tail = nxt
        in_specs += [
            pl.BlockSpec((1, 1, 3 * d), _mod_map(mod_n, tiles_per_seq)),
            pl.BlockSpec((1, d), lambda i: (0, 0)),
            pl.BlockSpec(w_n.shape, lambda i: (0, 0)),
        ]
        args += [mod_n, g_n, w_n]
        specs_n, shapes_n = _proj_out_specs(m, tm, w_n.shape[1], out_dtype, n_tail)
        out_specs += specs_n
        out_shape += shapes_n
    return pl.pallas_call(
        functools.partial(_outproj_kernel, d=d, with_next=nxt is not None),
        grid=(m // tm,),
        in_specs=in_specs,
        out_specs=out_specs,
        out_shape=out_shape,
        compiler_params=_params(("parallel",)),
        name="outproj",
    )(*args)


def _mlstm_pre_kernel(xm_ref, prev_ref, next_ref, cw_ref, cb_ref, bdq_ref, bdk_ref, bdv_ref, wg_ref, bg_ref,
                      q_ref, k_ref, v_ref, xc_ref, gc_ref, gr_ref, *, tiles_per_seq, n_heads, q_scale):
    i = pl.program_id(0)
    tm, inner = xm_ref.shape
    xm = xm_ref[...].astype(F32)
    row = lax.broadcasted_iota(jnp.int32, (tm, 1), 0)
    first = (i % tiles_per_seq) == 0
    last = (i % tiles_per_seq) == tiles_per_seq - 1
    halo = prev_ref.shape[0]
    prev_row = jnp.where(first, 0.0, prev_ref[halo - 1:halo, :].astype(F32))
    next_row = jnp.where(last, 0.0, next_ref[0:1, :].astype(F32))
    x_m1 = jnp.where(row == 0, prev_row, pltpu.roll(xm, 1, 0))
    x_p1 = jnp.where(row == tm - 1, next_row, pltpu.roll(xm, tm - 1, 0))
    conv = cw_ref[0:1, :] * x_m1 + cw_ref[1:2, :] * xm + cw_ref[2:3, :] * x_p1 + cb_ref[...]
    xc = _silu(conv)
    xc_ref[...] = xc.astype(xc_ref.dtype)

    xc_b = xc.astype(BF16)
    xm_b = xm.astype(BF16)
    blk = bdq_ref.shape[1]
    acc = jnp.zeros((tm, wg_ref.shape[1]), F32)
    for j in range(inner // blk):
        cols = slice(j * blk, (j + 1) * blk)
        qj = _dot(xc_b[:, cols], bdq_ref[j])
        kj = _dot(xc_b[:, cols], bdk_ref[j])
        vj = _dot(xm_b[:, cols], bdv_ref[j])
        qb, kb, vb = qj.astype(BF16), kj.astype(BF16), vj.astype(BF16)
        q_ref[:, cols] = (qj * q_scale).astype(BF16)
        k_ref[:, cols] = kb
        v_ref[:, cols] = vb
        acc = acc + _dot(qb, wg_ref[cols, :])
        acc = acc + _dot(kb, wg_ref[inner + j * blk:inner + (j + 1) * blk, :])
        acc = acc + _dot(vb, wg_ref[2 * inner + j * blk:2 * inner + (j + 1) * blk, :])
    g = acc + bg_ref[...]
    lane = lax.broadcasted_iota(jnp.int32, g.shape, 1)
    kind = lane % 4
    lf = jnp.where((kind >= 2) & (lane < 4 * n_heads), _log_sigmoid(g), 0.0)
    r = lax.broadcasted_iota(jnp.int32, (tm, tm), 0)
    c = lax.broadcasted_iota(jnp.int32, (tm, tm), 1)
    prefix = _dot_exact_lhs((c <= r).astype(BF16), lf)
    suffix = _dot_exact_lhs((c >= r).astype(BF16), lf)
    gcol = jnp.where(kind < 2, g, jnp.where(kind == 2, prefix, suffix))
    grow = jnp.transpose(gcol)
    for h in range(n_heads):
        gc_ref[h] = gcol[:, 4 * h:4 * h + 4]
        gr_ref[h] = grow[4 * h:4 * h + 4, :]


def _mlstm_pre(xz, conv_w, conv_b, bdq, bdk, bdv, wg, bg, seq_len, n_heads):
    m = xz.shape[0]
    inner = conv_w.shape[1]
    tm = MLSTM_CHUNK
    tiles_per_seq = seq_len // tm
    halo = 16
    last_halo = m // halo - 1
    return pl.pallas_call(
        functools.partial(_mlstm_pre_kernel, tiles_per_seq=tiles_per_seq, n_heads=n_heads,
                          q_scale=(inner // n_heads) ** -0.5),
        grid=(m // tm,),
        in_specs=[
            pl.BlockSpec((tm, inner), lambda i: (i, 0)),
            pl.BlockSpec((halo, inner), lambda i: (jnp.maximum(i * (tm // halo) - 1, 0), 0)),
            pl.BlockSpec((halo, inner), lambda i: (jnp.minimum((i + 1) * (tm // halo), last_halo), 0)),
            pl.BlockSpec(conv_w.shape, lambda i: (0, 0)),
            pl.BlockSpec((1, inner), lambda i: (0, 0)),
            pl.BlockSpec(bdq.shape, lambda i: (0, 0, 0)),
            pl.BlockSpec(bdk.shape, lambda i: (0, 0, 0)),
            pl.BlockSpec(bdv.shape, lambda i: (0, 0, 0)),
            pl.BlockSpec(wg.shape, lambda i: (0, 0)),
            pl.BlockSpec(bg.shape, lambda i: (0, 0)),
        ],
        out_specs=[
            pl.BlockSpec((tm, inner), lambda i: (i, 0)),
            pl.BlockSpec((tm, inner), lambda i: (i, 0)),
            pl.BlockSpec((tm, inner), lambda i: (i, 0)),
            pl.BlockSpec((tm, inner), lambda i: (i, 0)),
            pl.BlockSpec((n_heads, tm, 4), lambda i: (0, i, 0)),
            pl.BlockSpec((n_heads, 4, tm), lambda i: (0, 0, i)),
        ],
        out_shape=[
            jax.ShapeDtypeStruct((m, inner), BF16),
            jax.ShapeDtypeStruct((m, inner), BF16),
            jax.ShapeDtypeStruct((m, inner), BF16),
            jax.ShapeDtypeStruct((m, inner), BF16),
            jax.ShapeDtypeStruct((n_heads, m, 4), F32),
            jax.ShapeDtypeStruct((n_heads, 4, m), F32),
        ],
        compiler_params=_params(("parallel",)),
        name="mlstm_pre",
    )(xz, xz, xz, conv_w, conv_b, bdq, bdk, bdv, wg, bg)


def _mlstm_scan_kernel(*refs, n_chunks, has_state, emit_state, has_alias, state_slot):
    q_ref, k_ref, v_ref, xc_ref, gc_ref, gr_ref, og_ref, skip_ref = refs[:8]
    pos = 8
    if has_state:
        c0_ref, n0_ref, m0_ref = refs[pos:pos + 3]
        pos += 3
    pos += int(has_alias)
    u_ref = refs[pos]
    pos += 1
    if emit_state:
        cn_ref, nn_ref, mn_ref = refs[pos:pos + 3]
        pos += 3
        for other in range(cn_ref.shape[1]):
            if other != state_slot:
                cn_ref[0, other] = jnp.zeros(cn_ref.shape[2:], F32)
    c_scr, n_scr, m_scr, acc_scr, sqk_scr = refs[pos:pos + 5]

    L = MLSTM_CHUNK
    dh = q_ref.shape[1]
    row = lax.broadcasted_iota(jnp.int32, (L, L), 0)
    col = lax.broadcasted_iota(jnp.int32, (L, L), 1)

    def gates(rows, d):
        gc = gc_ref[0, rows, :]
        gr = gr_ref[0, :, rows]
        ig_col, b_col = gc[:, d:d + 1], gc[:, 2 + d:3 + d]
        ig_row, b_row = gr[d:d + 1, :], gr[2 + d:3 + d, :]
        mask = (col <= row) if d == 0 else (col >= row)
        dmat = jnp.where(mask, b_col - b_row + ig_row, -jnp.inf)
        b_last = b_col[L - 1:L, :] if d == 0 else b_col[0:1, :]
        return dmat, b_col, b_last - b_col + ig_col, b_last

    def finalize(rows, hs):
        mu = jnp.mean(hs, axis=-1, keepdims=True)
        hc = hs - mu
        hn = hc * lax.rsqrt(jnp.mean(hc * hc, axis=-1, keepdims=True) + EPS) * og_ref[...]
        u_ref[rows, :] = (hn + skip_ref[...] * xc_ref[rows, :].astype(F32)).astype(u_ref.dtype)

    if n_chunks == 1 and not has_state:
        rows = pl.ds(0, L)
        q, k, v = q_ref[...], k_ref[...], v_ref[...]
        s_qk = _dot_nt(q, k)
        p = None
        for d in range(2):
            dmat, b_col, w_state, b_last = gates(rows, d)
            m_t = jnp.max(dmat, axis=1, keepdims=True)
            s = s_qk * jnp.exp(dmat - m_t)
            den = jnp.sum(s, axis=1, keepdims=True)
            s = s / jnp.maximum(jnp.abs(den), jnp.exp(-m_t))
            p = s if p is None else p + s
            if emit_state:
                m_new = jnp.maximum(b_last, jnp.max(w_state, axis=0, keepdims=True))
                kw = k.astype(F32) * jnp.exp(w_state - m_new)
                cn_ref[0, state_slot, d, 0] = _dot_tn(kw.astype(BF16), v)
                nn_ref[0, d, 0] = jnp.sum(kw, axis=0, keepdims=True)
                mn_ref[0, d, 0] = jnp.broadcast_to(m_new, mn_ref.shape[3:])
        finalize(rows, _dot(p.astype(BF16), v))
        return

    assert n_chunks % 2 == 0

    def direction(c, d, finish, update):
        rows = pl.ds(pl.multiple_of(c * L, L), L)
        q = q_ref[rows, :]
        k = k_ref[rows, :]
        v = v_ref[rows, :]
        dmat, b_col, w_state, b_last = gates(rows, d)
        m_prev = m_scr[d]
        c_prev = c_scr[d]
        n_prev = n_scr[d]
        inter = b_col + m_prev
        m_t = jnp.maximum(inter, jnp.max(dmat, axis=1, keepdims=True))
        if finish:
            s_qk = sqk_scr[c]
        else:
            s_qk = _dot_nt(q, k)
            sqk_scr[c] = s_qk
        s = s_qk * jnp.exp(dmat - m_t)
        w_inter = jnp.exp(inter - m_t)
        num = _dot(s.astype(BF16), v) + w_inter * _dot(q, c_prev.astype(BF16))
        den = (jnp.sum(s, axis=1, keepdims=True)
               + w_inter * jnp.sum(q.astype(F32) * n_prev, axis=1, keepdims=True))
        h = num / jnp.maximum(jnp.abs(den), jnp.exp(-m_t))
        if finish:
            finalize(rows, acc_scr[rows, :] + h)
        else:
            acc_scr[rows, :] = h
        if update:
            m_new = jnp.maximum(b_last + m_prev, jnp.max(w_state, axis=0, keepdims=True))
            decay = jnp.exp(b_last + m_prev - m_new)
            kw = k.astype(F32) * jnp.exp(w_state - m_new)
            c_scr[d] = decay * c_prev + _dot_tn(kw.astype(BF16), v)
            n_scr[d] = decay * n_prev + jnp.sum(kw, axis=0, keepdims=True)
            m_scr[d] = m_new

    def step(s, finish, update):
        direction(s, 0, finish, update)
        direction(n_chunks - 1 - s, 1, finish, update)

    for d in range(2):
        if has_state:
            c_scr[d] = c0_ref[0, 0, d, 0]
            n_scr[d] = n0_ref[0, 0, d, 0]
            m_scr[d] = m0_ref[0, 0, d, 0][:, 0:1]
        else:
            c_scr[d] = jnp.zeros((dh, dh), F32)
            n_scr[d] = jnp.zeros((1, dh), F32)
            m_scr[d] = jnp.zeros((1, 1), F32)

    half = n_chunks // 2

    def first_half(s, carry):
        step(s, False, True)
        return carry

    def second_half(s, carry):
        step(s, True, True)
        return carry

    lax.fori_loop(0, half, first_half, 0)
    lax.fori_loop(half, n_chunks - 1, second_half, 0)
    step(n_chunks - 1, True, emit_state)
    if emit_state:
        for d in range(2):
            cn_ref[0, state_slot, d, 0] = c_scr[d]
            nn_ref[0, d, 0] = n_scr[d]
            mn_ref[0, d, 0] = jnp.broadcast_to(m_scr[d], mn_ref.shape[3:])


def _mlstm_scan(q, k, v, xc, gch, grh, out_g, skip, state, layer_idx, n_layers, c_stack, n_seq, seq_len, n_heads,
                emit_state):
    m, inner = q.shape
    dh = inner // n_heads
    n_chunks = seq_len // MLSTM_CHUNK
    has_state = state is not None
    j = layer_idx
    tok = lambda b, h: (b, h)
    in_specs = [
        pl.BlockSpec((seq_len, dh), tok),
        pl.BlockSpec((seq_len, dh), tok),
        pl.BlockSpec((seq_len, dh), tok),
        pl.BlockSpec((seq_len, dh), tok),
        pl.BlockSpec((1, seq_len, 4), lambda b, h: (h, b, 0)),
        pl.BlockSpec((1, 4, seq_len), lambda b, h: (h, 0, b)),
        pl.BlockSpec((1, dh), lambda b, h: (0, h)),
        pl.BlockSpec((1, dh), lambda b, h: (0, h)),
    ]
    args = [q, k, v, xc, gch, grh, out_g, skip]
    if has_state:
        c0, n0, m0 = state
        in_specs += [
            pl.BlockSpec((1, 1, 2, 1, dh, dh), lambda b, h: (b, j, 0, h, 0, 0)),
            pl.BlockSpec((1, 1, 2, 1, 1, dh), lambda b, h: (b, j, 0, h, 0, 0)),
            pl.BlockSpec((1, 1, 2, 1, 1, 128), lambda b, h: (b, j, 0, h, 0, 0)),
        ]
        args += [c0, n0, m0]
    out_specs = [pl.BlockSpec((seq_len, dh), tok)]
    out_shape = [jax.ShapeDtypeStruct((m, inner), BF16)]
    aliases = {}
    state_slot = 0
    if emit_state:
        if c_stack is None:
            c_spec = pl.BlockSpec((1, n_layers, 2, 1, dh, dh), lambda b, h: (b, 0, 0, h, 0, 0))
            state_slot = j
        else:
            c_spec = pl.BlockSpec((1, 1, 2, 1, dh, dh), lambda b, h: (b, j, 0, h, 0, 0))
        out_specs += [
            c_spec,
            pl.BlockSpec((1, 2, 1, 1, dh), lambda b, h: (b, 0, h, 0, 0)),
            pl.BlockSpec((1, 2, 1, 1, 128), lambda b, h: (b, 0, h, 0, 0)),
        ]
        out_shape += [
            jax.ShapeDtypeStruct((n_seq, n_layers, 2, n_heads, dh, dh), F32),
            jax.ShapeDtypeStruct((n_seq, 2, n_heads, 1, dh), F32),
            jax.ShapeDtypeStruct((n_seq, 2, n_heads, 1, 128), F32),
        ]
        if c_stack is not None:
            in_specs.append(pl.BlockSpec(memory_space=pl.ANY))
            args.append(c_stack)
            aliases = {len(args) - 1: 1}
    return pl.pallas_call(
        functools.partial(_mlstm_scan_kernel, n_chunks=n_chunks, has_state=has_state, emit_state=emit_state,
                          has_alias=bool(aliases), state_slot=state_slot),
        grid=(n_seq, n_heads),
        in_specs=in_specs,
        out_specs=out_specs,
        out_shape=out_shape,
        input_output_aliases=aliases,
        scratch_shapes=[pltpu.VMEM((2, dh, dh), F32), pltpu.VMEM((2, 1, dh), F32), pltpu.VMEM((2, 1, 1), F32),
                        pltpu.VMEM((seq_len if n_chunks > 1 else 8, dh), F32),
                        pltpu.VMEM((n_chunks if n_chunks > 1 else 1, MLSTM_CHUNK if n_chunks > 1 else 8,
                                    MLSTM_CHUNK), F32)],
        compiler_params=_params(("parallel", "parallel")),
        name="mlstm_scan",
    )(*args)


def _gla_scan_kernel(*refs, n_chunks, heads, has_state, emit_state):
    q_ref, k_ref, v_ref, a_ref, w2_ref, ba_ref, og_ref, dsel_ref = refs[:8]
    pos = 8
    if has_state:
        s0_ref = refs[pos]
        pos += 1
    o_ref = refs[pos]
    pos += 1
    if emit_state:
        sn_ref = refs[pos]
        pos += 1
    st_scr, acc_scr = refs[pos:pos + 2]

    L = GLA_CHUNK
    DB = GLA_DIAG
    dk = q_ref.shape[1] // heads
    dv = v_ref.shape[1] // heads
    qscale = dk ** -0.5
    row = lax.broadcasted_iota(jnp.int32, (L, L), 0)
    col = lax.broadcasted_iota(jnp.int32, (L, L), 1)
    sub = lax.broadcasted_iota(jnp.int32, (L // DB, DB, 1), 1)
    same_diag = (row // DB) == (col // DB)

    def direction(c, d, finish, update):
        rev = d == 1
        rows = pl.ds(pl.multiple_of(c * L, L), L)
        la = _log_sigmoid(_dot(a_ref[rows, :].astype(BF16), w2_ref[d]) + ba_ref[d]) * (LOG2E / GLA_TAU)
        tri = jnp.where((col >= row) if rev else (col <= row), 1.0, 0.0).astype(BF16)
        b_all = _dot_exact_lhs(tri, la)
        for hh in range(heads):
            kcols = slice(hh * dk, (hh + 1) * dk)
            vcols = slice(hh * dv, (hh + 1) * dv)
            q = q_ref[rows, kcols].astype(F32) * qscale
            k = k_ref[rows, kcols].astype(F32)
            v = v_ref[rows, vcols].astype(BF16)
            b = b_all[:, kcols]
            st = st_scr[d, hh]

            o = _dot_nt((q * jnp.exp2(b)).astype(BF16), st.astype(BF16))

            att = jnp.zeros((L, L), F32)
            blk = L // 2
            while blk >= DB:
                pieces = []
                for g in range(L // (2 * blk)):
                    idx = g * 2 * blk + (blk if rev else blk - 1)
                    pieces.append(jnp.broadcast_to(b[idx:idx + 1, :], (2 * blk, dk)))
                ref_rows = pieces[0] if len(pieces) == 1 else jnp.concatenate(pieces, axis=0)
                e = jnp.exp2(-jnp.abs(b - ref_rows))
                a_lvl = _dot_nt((q * e).astype(BF16), (k * e).astype(BF16))
                rb, cb = row // blk, col // blk
                if rev:
                    pair = ((rb % 2) == 0) & (cb == rb + 1)
                else:
                    pair = ((rb % 2) == 1) & (cb == rb - 1)
                att = jnp.where(pair, a_lvl, att)
                blk //= 2

            q3 = q.reshape(L // DB, DB, dk)
            k3 = k.reshape(L // DB, DB, dk)
            b3 = b.reshape(L // DB, DB, dk)
            parts = []
            for j in range(DB):
                valid = (sub <= j) if rev else (sub >= j)
                x = q3 * k3[:, j:j + 1, :] * jnp.exp2(jnp.where(valid, b3 - b3[:, j:j + 1, :], -jnp.inf))
                parts.append(x.reshape(L, dk).astype(BF16))
            diag = _dot(jnp.concatenate(parts, axis=1), dsel_ref[...])
            att = jnp.where(same_diag, diag, att)

            o = o + _dot(att.astype(BF16), v)
            if not finish:
                acc_scr[rows, vcols] = o
            else:
                os_ = acc_scr[rows, vcols] + o
                o_ref[rows, vcols] = (os_ * lax.rsqrt(jnp.mean(os_ * os_, axis=-1, keepdims=True) + EPS)
                                      * og_ref[:, vcols]).astype(o_ref.dtype)
            if update:
                b_last = b[0:1, :] if rev else b[L - 1:L, :]
                kt = (k * jnp.exp2(b_last - b)).astype(BF16)
                st_scr[d, hh] = st * jnp.exp2(b_last) + _dot_tn(v, kt)

    def step(s, finish, update):
        direction(s, 0, finish, update)
        direction(n_chunks - 1 - s, 1, finish, update)

    for d in range(2):
        for hh in range(heads):
            if has_state:
                st_scr[d, hh] = jnp.transpose(s0_ref[0, 0, d, hh])
            else:
                st_scr[d, hh] = jnp.zeros((dv, dk), F32)

    half = n_chunks // 2

    def first_half(s, carry):
        step(s, False, True)
        return carry

    def second_half(s, carry):
        step(s, True, True)
        return carry

    lax.fori_loop(0, half, first_half, 0)
    lax.fori_loop(half, n_chunks - 1, second_half, 0)
    step(n_chunks - 1, True, emit_state)
    if emit_state:
        for d in range(2):
            for hh in range(heads):
                sn_ref[0, 0, d, hh] = jnp.transpose(st_scr[d, hh])


def _gla_scan(proj, aproj, w2pad, b_a, out_g, dsel, state, n_seq, seq_len, n_heads, dk_total, dv_total, emit_state):
    m = proj.shape[0]
    heads = GLA_HEADS_PER_STEP
    dk = dk_total // n_heads
    dv = dv_total // n_heads
    n_chunks = seq_len // GLA_CHUNK
    assert n_chunks % 2 == 0 and n_heads % heads == 0
    has_state = state is not None
    wk, wv = heads * dk, heads * dv
    in_specs = [
        pl.BlockSpec((seq_len, wk), lambda b, h: (b, h)),
        pl.BlockSpec((seq_len, wk), lambda b, h: (b, dk_total // wk + h)),
        pl.BlockSpec((seq_len, wv), lambda b, h: (b, 2 * dk_total // wv + h)),
        pl.BlockSpec((seq_len, 128), lambda b, h: (b, 0)),
        pl.BlockSpec((2, 128, wk), lambda b, h: (0, 0, h)),
        pl.BlockSpec((2, 1, wk), lambda b, h: (0, 0, h)),
        pl.BlockSpec((1, wv), lambda b, h: (0, h)),
        pl.BlockSpec(dsel.shape, lambda b, h: (0, 0)),
    ]
    args = [proj, proj, proj, aproj, w2pad, b_a, out_g, dsel]
    if has_state:
        in_specs.append(pl.BlockSpec((1, 1, 2, heads, dk, dv), lambda b, h: (b, 0, 0, h, 0, 0)))
        args.append(state)
    out_specs = [pl.BlockSpec((seq_len, wv), lambda b, h: (b, h))]
    out_shape = [jax.ShapeDtypeStruct((m, dv_total), BF16)]
    if emit_state:
        out_specs.append(pl.BlockSpec((1, 1, 2, heads, dk, dv), lambda b, h: (b, 0, 0, h, 0, 0)))
        out_shape.append(jax.ShapeDtypeStruct((n_seq, 1, 2, n_heads, dk, dv), F32))
    return pl.pallas_call(
        functools.partial(_gla_scan_kernel, n_chunks=n_chunks, heads=heads, has_state=has_state,
                          emit_state=emit_state),
        grid=(n_seq, n_heads // heads),
        in_specs=in_specs,
        out_specs=out_specs,
        out_shape=out_shape,
        scratch_shapes=[pltpu.VMEM((2, heads, dv, dk), F32), pltpu.VMEM((seq_len, wv), F32)],
        compiler_params=_params(("parallel", "parallel")),
        name="gla_scan",
    )(*args)


def _pair_rmsnorm(x, g, lane_lo):
    w = x.shape[-1]
    r = lax.broadcasted_iota(jnp.int32, (w, w), 0) < w // 2
    c = lax.broadcasted_iota(jnp.int32, (w, w), 1) < w // 2
    same_half = jnp.where(r == c, 1.0, 0.0).astype(BF16)
    x2 = x * x
    hi = x2.astype(BF16)
    lo = (x2 - hi.astype(F32)).astype(BF16)
    ms = (_dot(hi, same_half) + _dot(lo, same_half)) * (2.0 / w)
    return x * lax.rsqrt(ms + EPS) * g


def _na_ctx_kernel(q_ref, k_ref, v_ref, qg_ref, kg_ref, o_ref, kn_ref, vn_ref):
    w = qg_ref.shape[1]
    dh = w // 2
    scale = dh ** -0.5
    lane_lo = lax.broadcasted_iota(jnp.int32, (1, w), 1) < dh
    for pair in range(q_ref.shape[1] // w):
        cols = slice(pair * w, (pair + 1) * w)
        qn = _pair_rmsnorm(q_ref[:, cols].astype(F32), qg_ref[...], lane_lo)
        kn = _pair_rmsnorm(k_ref[:, cols].astype(F32), kg_ref[...], lane_lo)
        v = v_ref[:, cols].astype(F32)
        for hh in range(2):
            kn_ref[0, 0, 2 * pair + hh] = kn[:, hh * dh:(hh + 1) * dh]
            vn_ref[0, 0, 2 * pair + hh] = v[:, hh * dh:(hh + 1) * dh]
        kb = kn.astype(BF16)
        vb = v.astype(BF16)
        outs = []
        for lo in (True, False):
            sel = lane_lo if lo else jnp.logical_not(lane_lo)
            qh = jnp.where(sel, qn, 0.0).astype(BF16)
            s = _dot_nt(qh, kb) * scale
            p = jnp.exp(s - jnp.max(s, axis=-1, keepdims=True))
            l = jnp.sum(p, axis=-1, keepdims=True)
            outs.append(_dot(p.astype(BF16), vb) / l)
        o_ref[:, cols] = jnp.where(lane_lo, outs[0], outs[1]).astype(o_ref.dtype)


def _na_context(proj, q_g2, k_g2, n_seq, seq_len, d_model):
    m = proj.shape[0]
    dh = q_g2.shape[1] // 2
    wb = NA_CTX_PAIRS * 2 * dh
    nb = d_model // wb
    return pl.pallas_call(
        _na_ctx_kernel,
        grid=(n_seq, nb),
        in_specs=[
            pl.BlockSpec((seq_len, wb), lambda b, p: (b, p)),
            pl.BlockSpec((seq_len, wb), lambda b, p: (b, nb + p)),
            pl.BlockSpec((seq_len, wb), lambda b, p: (b, 2 * nb + p)),
            pl.BlockSpec((1, 2 * dh), lambda b, p: (0, 0)),
            pl.BlockSpec((1, 2 * dh), lambda b, p: (0, 0)),
        ],
        out_specs=[
            pl.BlockSpec((seq_len, wb), lambda b, p: (b, p)),
            pl.BlockSpec((1, 1, 2 * NA_CTX_PAIRS, seq_len, dh), lambda b, p: (b, 0, p, 0, 0)),
            pl.BlockSpec((1, 1, 2 * NA_CTX_PAIRS, seq_len, dh), lambda b, p: (b, 0, p, 0, 0)),
        ],
        out_shape=[jax.ShapeDtypeStruct((m, d_model), BF16),
                   jax.ShapeDtypeStruct((n_seq, 1, d_model // dh, seq_len, dh), F32),
                   jax.ShapeDtypeStruct((n_seq, 1, d_model // dh, seq_len, dh), F32)],
        compiler_params=_params(("parallel", "parallel")),
        name="na_context",
    )(proj, proj, proj, q_g2, k_g2)


def _na_latent_kernel(q_ref, k_ref, v_ref, kc_ref, vc_ref, bias_ref, qg_ref, kg_ref, o_ref, qn_scr, kn_scr, vb_scr,
                      kc_scr, vc_scr, s_scr, m_scr, p_scr, l_scr, *, n_rows, win_h):
    t, w = q_ref.shape
    dh = w // 2
    gw = t // n_rows
    lane_lo = lax.broadcasted_iota(jnp.int32, (1, w), 1) < dh
    qn_scr[...] = _pair_rmsnorm(q_ref[...].astype(F32), qg_ref[...], lane_lo) * (dh ** -0.5 * LOG2E)
    kn_scr[...] = _pair_rmsnorm(k_ref[...].astype(F32), kg_ref[...], lane_lo).astype(BF16)
    vb_scr[...] = v_ref[...].astype(BF16)
    kc_scr[...] = jnp.concatenate([kc_ref[0, 0], kc_ref[0, 1]], axis=1).astype(BF16)
    vc_scr[...] = jnp.concatenate([vc_ref[0, 0], vc_ref[0, 1]], axis=1).astype(BF16)
    n_loc = win_h * gw

    def window(r):
        r0 = jnp.clip(r - win_h // 2, 0, n_rows - win_h)
        return r0, pl.ds(pl.multiple_of(r0 * gw, gw), n_loc)

    def scores(r, slot):
        r0, krows = window(r)
        qr = qn_scr[pl.ds(pl.multiple_of(r * gw, gw), gw), :]
        q2 = jnp.concatenate([jnp.where(lane_lo, qr, 0.0), jnp.where(lane_lo, 0.0, qr)], axis=0).astype(BF16)
        shift = NA_WIN_H - 1 - (r - r0)
        start = pl.multiple_of((shift // 2) * (2 * gw), 2 * gw)
        s_loc = _dot_nt(q2, kn_scr[krows, :]) + bias_ref[shift % 2, 0, :, pl.ds(start, n_loc)]
        s_ctx = _dot_nt(q2, kc_scr[...])
        s_scr[slot, :, 0:n_loc] = s_loc
        s_scr[slot, :, n_loc:] = s_ctx
        m_scr[slot] = jnp.maximum(jnp.max(s_loc, axis=-1, keepdims=True), jnp.max(s_ctx, axis=-1, keepdims=True))

    def probabilities(slot):
        p = jnp.exp2(s_scr[slot] - m_scr[slot])
        l_scr[slot] = jnp.sum(p, axis=-1, keepdims=True)
        p_scr[slot] = p.astype(BF16)

    def output(r, slot):
        _, krows = window(r)
        o2 = (_dot(p_scr[slot, :, 0:n_loc], vb_scr[krows, :]) + _dot(p_scr[slot, :, n_loc:], vc_scr[...])) / l_scr[slot]
        o_ref[pl.ds(pl.multiple_of(r * gw, gw), gw), :] = jnp.where(lane_lo, o2[0:gw], o2[gw:]).astype(o_ref.dtype)

    scores(0, 0)
    scores(1, 1)
    probabilities(0)

    def body(ii, carry):
        for slot in range(2):
            i = 2 * ii + slot
            output(i, slot)
            probabilities(1 - slot)
            scores(i + 2, slot)
        return carry

    lax.fori_loop(0, n_rows // 2 - 1, body, 0)
    output(n_rows - 2, 0)
    probabilities(1)
    output(n_rows - 1, 1)


def _na_latent(proj, kc, vc, bias, q_g2, k_g2, n_seq, seq_len, d_model):
    m = proj.shape[0]
    nb = d_model // 128
    n_rows = seq_len // GRID_W
    win_h = min(NA_WIN_H, n_rows)
    past, dh = kc.shape[2:]
    n_keys = win_h * GRID_W + past
    assert n_rows % 2 == 0 and n_rows >= 4
    return pl.pallas_call(
        functools.partial(_na_latent_kernel, n_rows=n_rows, win_h=win_h),
        grid=(n_seq, nb),
        in_specs=[
            pl.BlockSpec((seq_len, 128), lambda b, p: (b, p)),
            pl.BlockSpec((seq_len, 128), lambda b, p: (b, nb + p)),
            pl.BlockSpec((seq_len, 128), lambda b, p: (b, 2 * nb + p)),
            pl.BlockSpec((1, 2, past, dh), lambda b, p: (b, p, 0, 0)),
            pl.BlockSpec((1, 2, past, dh), lambda b, p: (b, p, 0, 0)),
            pl.BlockSpec((2, 1) + bias.shape[2:], lambda b, p: (0, p, 0, 0)),
            pl.BlockSpec((1, 128), lambda b, p: (0, 0)),
            pl.BlockSpec((1, 128), lambda b, p: (0, 0)),
        ],
        out_specs=pl.BlockSpec((seq_len, 128), lambda b, p: (b, p)),
        out_shape=jax.ShapeDtypeStruct((m, d_model), BF16),
        scratch_shapes=[pltpu.VMEM((seq_len, 128), F32), pltpu.VMEM((seq_len, 128), BF16),
                        pltpu.VMEM((seq_len, 128), BF16), pltpu.VMEM((past, 128), BF16),
                        pltpu.VMEM((past, 128), BF16), pltpu.VMEM((2, 2 * GRID_W, n_keys), F32),
                        pltpu.VMEM((2, 2 * GRID_W, 1), F32), pltpu.VMEM((2, 2 * GRID_W, n_keys), BF16),
                        pltpu.VMEM((2, 2 * GRID_W, 1), F32)],
        compiler_params=_params(("parallel", "parallel")),
        name="na_latent",
    )(proj, proj, proj, kc, vc, bias, q_g2, k_g2)


def _na_bias_table(rpb, n_rows):
    win_h = min(NA_WIN_H, n_rows)
    n_h, n_rr, n_cr = rpb.shape
    cols = jnp.arange(GRID_W)
    col_start = jnp.clip(cols - NA_WIN_W // 2, 0, GRID_W - NA_WIN_W)
    in_win = (cols[None, :] >= col_start[:, None]) & (cols[None, :] < col_start[:, None] + NA_WIN_W)
    period = GRID_W + 1
    u = jnp.concatenate([rpb[:, :, NA_WIN_W - 1:], jnp.zeros((n_h, n_rr, period - n_cr), rpb.dtype),
                         rpb[:, :, :NA_WIN_W - 1]], axis=2)
    toe = jnp.tile(u, (1, 1, GRID_W))[:, :, :GRID_W * GRID_W].reshape(n_h, n_rr, GRID_W, GRID_W)
    toe = toe.transpose(0, 2, 1, 3)
    toe = jnp.where(in_win[None, :, None, :], toe * LOG2E, -jnp.inf).astype(F32)
    wide = toe.reshape(n_h // 2, 2 * GRID_W, n_rr * GRID_W)
    lanes = (NA_WIN_H - 1 + win_h + 1) // 2 * 2 * GRID_W
    pad = lambda a: jnp.pad(a, ((0, 0), (0, 0), (0, lanes - a.shape[2])))
    return jnp.stack([pad(wide), pad(wide[:, :, GRID_W:])], axis=0)


def _block_diag_tiles(w, tile):
    n_blk, s, _ = w.shape
    per = tile // s
    n_tiles = n_blk // per
    rows = w.reshape(n_tiles, per, s, s).transpose(0, 2, 1, 3).reshape(n_tiles, 1, s, tile)
    full = jnp.broadcast_to(rows, (n_tiles, per, s, tile)).reshape(n_tiles, tile, tile)
    blk = jnp.arange(tile) // s
    return jnp.where((blk[:, None] == blk[None, :])[None], full, 0.0)


def _mlstm_layer(streams, projs, p, state, j, n_layers, c_stack):
    (conv_w, conv_b, wq, wk, wv, w_ig, b_ig, w_fg, b_fg, out_g, skip, w_out) = p
    inner = conv_w.shape[1]
    n_heads = b_ig.shape[1]
    w_out_b = w_out.astype(BF16)
    bdq = _block_diag_tiles(wq, 256).astype(BF16)
    bdk = _block_diag_tiles(wk, 256).astype(BF16)
    bdv = _block_diag_tiles(wv, 256).astype(BF16)
    n_g = 4 * n_heads
    wg = jnp.stack([w_ig[0], w_ig[1], w_fg[0], w_fg[1]], axis=2).reshape(3 * inner, n_g)
    wg = jnp.pad(wg, ((0, 0), (0, 128 - n_g))).astype(BF16)
    bg = jnp.pad(jnp.stack([b_ig[0], b_ig[1], b_fg[0], b_fg[1]], axis=1).reshape(n_g), (0, 128 - n_g)).reshape(1, 128)
    outs, new_state = [], None
    for (n_seq, seq_len, is_prompt), (xz,) in zip(streams, projs):
        q, k, v, xc, gch, grh = _mlstm_pre(xz, conv_w, conv_b.reshape(1, inner), bdq, bdk, bdv, wg, bg, seq_len,
                                           n_heads)
        st = None
        if not is_prompt:
            c0, n0, m0 = state
            st = (c0, n0.reshape(n0.shape[:4] + (1, n0.shape[4])),
                  jnp.broadcast_to(m0[..., None, None], m0.shape + (1, 128)))
        res = _mlstm_scan(q, k, v, xc, gch, grh, out_g.reshape(1, inner), skip.reshape(1, inner), st, j, n_layers,
                          c_stack, n_seq, seq_len, n_heads, emit_state=is_prompt)
        if is_prompt:
            u, cn, nn, mn = res
            new_state = (cn, nn[:, :, :, 0, :], mn[:, :, :, 0, 0])
        else:
            u = res[0]
        outs.append((u, xz, 1))
    return outs, new_state, w_out_b


def _gla_in_weights(w_in, w_a1):
    pad = 128 - 2 * GLA_RANK
    return jnp.concatenate([w_in, w_a1[0], w_a1[1], jnp.zeros((w_in.shape[0], pad), F32)], axis=1).astype(BF16)


def _gla_layer(streams, projs, p, state):
    (w_a2, b_a, out_g, w_out) = p
    n_heads = state.shape[3]
    dk_total = w_a2.shape[2]
    dv_total = out_g.shape[0]
    w2pad = jnp.zeros((2, 128, dk_total), F32)
    w2pad = w2pad.at[0, 0:GLA_RANK].set(w_a2[0]).at[1, GLA_RANK:2 * GLA_RANK].set(w_a2[1]).astype(BF16)
    w_out_b = w_out.astype(BF16)
    L, DB = GLA_CHUNK, GLA_DIAG
    dk = dk_total // n_heads
    dsel = (jnp.arange(DB * dk)[:, None] // dk == jnp.arange(L)[None, :] % DB).astype(BF16)
    outs, new_state = [], None
    for (n_seq, seq_len, is_prompt), (proj, aproj) in zip(streams, projs):
        res = _gla_scan(proj, aproj, w2pad, b_a.reshape(2, 1, dk_total), out_g.reshape(1, dv_total), dsel,
                        None if is_prompt else state, n_seq, seq_len, n_heads, dk_total, dv_total,
                        emit_state=is_prompt)
        if is_prompt:
            new_state = res[1]
        outs.append((res[0], proj, (2 * dk_total + dv_total) // dv_total))
    return outs, new_state, w_out_b


def _na_layer(streams, projs, p, cache_k, cache_v):
    (q_g, k_g, rpb, w_out) = p
    d = w_out.shape[0]
    n_heads = rpb.shape[0]
    dh = d // n_heads
    w_out_b = w_out.astype(BF16)
    q_g2 = jnp.tile(q_g, 2).reshape(1, 2 * dh)
    k_g2 = jnp.tile(k_g, 2).reshape(1, 2 * dh)
    outs, new_kv = [], None
    for (n_seq, seq_len, is_prompt), (proj,) in zip(streams, projs):
        if is_prompt:
            o, kn, vn = _na_context(proj, q_g2, k_g2, n_seq, seq_len, d)
            new_kv = (kn, vn)
        else:
            bias = _na_bias_table(rpb, seq_len // GRID_W)
            o = _na_latent(proj, cache_k, cache_v, bias, q_g2, k_g2, n_seq, seq_len, d)
        outs.append((o, proj, 3))
    return outs, new_kv, w_out_b


def kernel(x_prompt, x_sample, state_mlstm_C, state_mlstm_n, state_mlstm_m, state_gla_S, cache_na_k, cache_na_v, c, c_ctx, norm_g, w_mod, b_mod, a_w_in, a_conv_w, a_conv_b, a_wq, a_wk, a_wv, a_w_ig, a_b_ig, a_w_fg, a_b_fg, a_out_g, a_skip, a_w_out, b_w_in, b_w_a1, b_w_a2, b_b_a, b_out_g, b_w_out, c_w_in, c_q_g, c_k_g, c_rpb, c_w_out):
    bp, seq, d = x_prompt.shape
    bs, dec_seq, _ = x_sample.shape
    depth = w_mod.shape[0]
    n_mlstm = state_mlstm_C.shape[1]

    cs = jnp.zeros((16, d), F32).at[0:bs].set(c).at[bs].set(c_ctx)
    mod_all = _modulation(cs, w_mod, b_mod)

    streams = [(bp, seq, True), (bs, dec_seq, False)]
    ys_ = [x_prompt.reshape(bp * seq, d), x_sample.reshape(bs * dec_seq, d)]

    def mods(l):
        return [mod_all[l, bs:bs + 1].reshape(1, 1, 3 * d), mod_all[l, 0:bs].reshape(bs, 1, 3 * d)]

    in_weights = {}

    def in_projection(l, is_prompt):
        kind, j = l % 3, l // 3
        if l not in in_weights:
            in_weights[l] = (_gla_in_weights(b_w_in[j], b_w_a1[j]) if kind == 1
                             else (a_w_in, c_w_in)[kind // 2][j].astype(BF16))
        out_dtype = F32 if (kind == 2 and is_prompt) else BF16
        return in_weights[l], out_dtype, 128 if kind == 1 else 0

    projs = []
    for i, (n_seq, seq_len, is_prompt) in enumerate(streams):
        w0, dt0, tail0 = in_projection(0, is_prompt)
        projs.append(_inproj(ys_[i], mods(0)[i], norm_g[0].reshape(1, d), w0, seq_len, dt0, tail0))

    c_stack = None
    new_n, new_m, new_S, new_k, new_v = [], [], [], [], []
    for l in range(depth):
        kind, j = l % 3, l // 3
        if kind == 0:
            pa = (a_conv_w[j], a_conv_b[j], a_wq[j], a_wk[j], a_wv[j], a_w_ig[j], a_b_ig[j],
                  a_w_fg[j], a_b_fg[j], a_out_g[j], a_skip[j], a_w_out[j])
            outs, (c_stack, nn, mn), w_out_b = _mlstm_layer(
                streams, projs, pa, (state_mlstm_C, state_mlstm_n, state_mlstm_m), j, n_mlstm, c_stack)
            new_n.append(nn)
            new_m.append(mn)
        elif kind == 1:
            pb = (b_w_a2[j], b_b_a[j], b_out_g[j], b_w_out[j])
            outs, sn, w_out_b = _gla_layer(streams, projs, pb, state_gla_S[:, j:j + 1])
            new_S.append(sn)
        else:
            pc = (c_q_g[j], c_k_g[j], c_rpb[j], c_w_out[j])
            outs, (kn, vn), w_out_b = _na_layer(streams, projs, pc, cache_na_k[:, j], cache_na_v[:, j])
            new_k.append(kn)
            new_v.append(vn)
        projs = []
        for i, ((n_seq, seq_len, is_prompt), (u, r_arr, r_blk)) in enumerate(zip(streams, outs)):
            nxt = None
            if l + 1 < depth:
                w_n, dt_n, tail_n = in_projection(l + 1, is_prompt)
                nxt = (mods(l + 1)[i], norm_g[l + 1].reshape(1, d), w_n, dt_n, tail_n)
            res = _outproj(u, r_arr, r_blk, mods(l)[i], w_out_b, ys_[i], seq_len, nxt)
            ys_[i] = res[0]
            projs.append(tuple(res[1:]))
    yp, ys = ys_
    return (yp.reshape(bp, seq, d), ys.reshape(bs, dec_seq, d), c_stack, jnp.stack(new_n, axis=1),
            jnp.stack(new_m, axis=1), jnp.concatenate(new_S, axis=1), jnp.concatenate(new_k, axis=1),
            jnp.concatenate(new_v, axis=1))
```
